```python
import math
import jax, jax.numpy as jnp
from jax import lax
import numpy as np

D_MODEL = 1024
BATCH = 32
SEQ = 2048
DEPTH = 1

GLA_HEADS = 4
GLA_DK = 64
GLA_DV = 128
GLA_GATE_RANK = 16
GLA_GATE_NORMALIZER = 16.0
GLA_CHUNK = 64
DIFF_HEADS = 4
DIFF_DK = 64
DIFF_DV = 128
Q_BLOCK = 128
REL_BUCKETS = 32
REL_MAX_DIST = 128
D_FF = -(-8 * D_MODEL // (3 * 256)) * 256
EPS = 1e-6

GLA_Q_COLS = GLA_HEADS * GLA_DK
GLA_K_COLS = GLA_HEADS * GLA_DK
GLA_V_COLS = GLA_HEADS * GLA_DV
GLA_OG_COLS = GLA_HEADS * GLA_DV
DIFF_Q_COLS = DIFF_HEADS * 2 * DIFF_DK
DIFF_K_COLS = DIFF_HEADS * 2 * DIFF_DK
DIFF_V_COLS = DIFF_HEADS * DIFF_DV
MIX_WIDTH = GLA_HEADS * GLA_DV + DIFF_HEADS * DIFF_DV
IN_COLS = (GLA_Q_COLS + GLA_K_COLS + GLA_V_COLS + GLA_OG_COLS + GLA_GATE_RANK
           + DIFF_Q_COLS + DIFF_K_COLS + DIFF_V_COLS)

kernel_name = "hybrid_gla_diffattn_t5bias_sandwich"


def rms_norm(x, g):
    xf = x.astype(jnp.float32)
    y = xf * lax.rsqrt(jnp.mean(xf * xf, axis=-1, keepdims=True) + EPS)
    return (y * g.astype(jnp.float32)).astype(x.dtype)


def split_columns(proj):
    sizes = [GLA_Q_COLS, GLA_K_COLS, GLA_V_COLS, GLA_OG_COLS, GLA_GATE_RANK,
             DIFF_Q_COLS, DIFF_K_COLS, DIFF_V_COLS]
    points = [int(p) for p in np.cumsum(sizes)[:-1]]
    return jnp.split(proj, points, axis=-1)


def gla_mixer(q, k, v, log_a, og, g_head):
    B, S = q.shape[0], q.shape[1]
    C = GLA_CHUNK
    N = S // C
    f32 = jnp.float32

    def chunk(t):
        return t.astype(f32).reshape(B, N, C, t.shape[2], t.shape[3]).transpose(0, 3, 1, 2, 4)

    qc = chunk(q) * (GLA_DK ** -0.5)
    kc, vc, ac = chunk(k), chunk(v), chunk(log_a)
    b = jnp.cumsum(ac, axis=3)
    b_last = b[:, :, :, C - 1:C, :]
    b_ref = b[:, :, :, C // 2 - 1:C // 2, :]

    scores = jnp.einsum('bhnid,bhnjd->bhnij', qc * jnp.exp(b - b_ref), kc * jnp.exp(b_ref - b))
    causal = jnp.tril(jnp.ones((C, C), dtype=bool))
    scores = jnp.where(causal, scores, 0.0)
    o_intra = jnp.einsum('bhnij,bhnjv->bhniv', scores, vc)

    u = jnp.einsum('bhnjd,bhnjv->bhndv', kc * jnp.exp(b_last - b), vc)
    decay = jnp.exp(b_last[:, :, :, 0, :])

    def step(state, inp):
        d_n, u_n = inp
        return d_n[..., None] * state + u_n, state

    s0 = jnp.zeros((B, q.shape[2], GLA_DK, GLA_DV), f32)
    _, s_prev = lax.scan(step, s0, (decay.transpose(2, 0, 1, 3), u.transpose(2, 0, 1, 3, 4)))
    s_prev = s_prev.transpose(1, 2, 0, 3, 4)
    o_inter = jnp.einsum('bhnid,bhndv->bhniv', qc * jnp.exp(b), s_prev)

    o = (o_intra + o_inter).transpose(0, 2, 3, 1, 4).reshape(B, S, q.shape[2], GLA_DV)
    o = rms_norm(o, g_head) * jax.nn.silu(og.astype(f32))
    return o.reshape(B, S, q.shape[2] * GLA_DV).astype(v.dtype)


def t5_causal_bucket(rel):
    n = jnp.maximum(rel, 0)
    max_exact = REL_BUCKETS // 2
    nf = jnp.maximum(n, 1).astype(jnp.float32)
    large = max_exact + (jnp.log(nf / max_exact) / math.log(REL_MAX_DIST / max_exact)
                         * (REL_BUCKETS - max_exact)).astype(jnp.int32)
    large = jnp.minimum(large, REL_BUCKETS - 1)
    return jnp.where(n < max_exact, n, large)


def diff_attention(q, k, v, lam, rel_table, g_head, lambda_init):
    B, S, H = q.shape[0], q.shape[1], q.shape[2]
    qh = (q * (DIFF_DK ** -0.5)).transpose(0, 2, 3, 1, 4)
    kh = k.transpose(0, 2, 3, 1, 4)
    vh = v.transpose(0, 2, 1, 3)
    lam32 = lam.astype(jnp.float32)
    outs = []
    for qb in range(S // Q_BLOCK):
        q0 = qb * Q_BLOCK
        kv_len = q0 + Q_BLOCK
        logits = jnp.einsum('bhmqd,bhmkd->bhmqk', qh[:, :, :, q0:kv_len],
                            kh[:, :, :, :kv_len]).astype(jnp.float32)
        q_pos = jnp.arange(q0, kv_len)
        k_pos = jnp.arange(kv_len)
        rel = q_pos[:, None] - k_pos[None, :]
        bias = rel_table.astype(jnp.float32)[t5_causal_bucket(rel)].transpose(2, 0, 1)
        logits = logits + bias[None, :, None]
        logits = jnp.where((rel >= 0)[None, None, None], logits, -jnp.inf)
        p = jax.nn.softmax(logits, axis=-1)
        a = p[:, :, 0] - lam32 * p[:, :, 1]
        outs.append(jnp.einsum('bhqk,bhkv->bhqv', a.astype(v.dtype), vh[:, :, :kv_len]))
    o = jnp.concatenate(outs, axis=2)
    o = rms_norm(o, g_head) * (1.0 - lambda_init)
    return o.transpose(0, 2, 1, 3).reshape(B, S, H * DIFF_DV).astype(v.dtype)


def setup_inputs(seed: int = 0) -> dict:
    key = jax.random.key(seed)
    ks = jax.random.split(key, 20)
    f32 = jnp.float32

    def nrm(k, shape, scale):
        return jax.random.normal(k, shape, f32) * scale

    def gain(k, shape):
        return jnp.ones(shape, f32) + 0.01 * jax.random.normal(k, shape, f32)

    return {
        "x": jax.random.normal(ks[0], (BATCH, SEQ, D_MODEL), f32),
        "w_in": nrm(ks[1], (DEPTH, D_MODEL, IN_COLS), D_MODEL ** -0.5),
        "w_gk_up": nrm(ks[2], (DEPTH, GLA_GATE_RANK, GLA_HEADS * GLA_DK), GLA_GATE_RANK ** -0.5),
        "b_gk": nrm(ks[3], (DEPTH, GLA_HEADS * GLA_DK), 0.01),
        "lambda_q1": nrm(ks[4], (DEPTH, DIFF_DK), 0.1),
        "lambda_k1": nrm(ks[5], (DEPTH, DIFF_DK), 0.1),
        "lambda_q2": nrm(ks[6], (DEPTH, DIFF_DK), 0.1),
        "lambda_k2": nrm(ks[7], (DEPTH, DIFF_DK), 0.1),
        "rel_bias": nrm(ks[8], (REL_BUCKETS, DIFF_HEADS), 0.5),
        "g_gla_head": gain(ks[9], (DEPTH, GLA_DV)),
        "g_diff_head": gain(ks[10], (DEPTH, DIFF_DV)),
        "w_out": nrm(ks[11], (DEPTH, MIX_WIDTH, D_MODEL), MIX_WIDTH ** -0.5),
        "pre_mix_g": gain(ks[12], (DEPTH, D_MODEL)),
        "post_mix_g": gain(ks[13], (DEPTH, D_MODEL)),
        "w_gate": nrm(ks[14], (DEPTH, D_MODEL, D_FF), D_MODEL ** -0.5),
        "w_up": nrm(ks[15], (DEPTH, D_MODEL, D_FF), D_MODEL ** -0.5),
        "w_down": nrm(ks[16], (DEPTH, D_FF, D_MODEL), D_FF ** -0.5),
        "pre_ffn_g": gain(ks[17], (DEPTH, D_MODEL)),
        "post_ffn_g": gain(ks[18], (DEPTH, D_MODEL)),
    }


def reference(x, w_in, w_gk_up, b_gk, lambda_q1, lambda_k1, lambda_q2, lambda_k2,
              rel_bias, g_gla_head, g_diff_head, w_out, pre_mix_g, post_mix_g,
              w_gate, w_up, w_down, pre_ffn_g, post_ffn_g):
    B, S, _ = x.shape
    for l in range(DEPTH):
        lambda_init = 0.8 - 0.6 * math.exp(-0.3 * l)
        h = rms_norm(x, pre_mix_g[l])
        proj = jnp.einsum('bsd,dc->bsc', h, w_in[l])
        gq, gk, gv, gog, glow, dq, dk, dv = split_columns(proj)
        log_a = jax.nn.log_sigmoid(
            (jnp.einsum('bsr,rc->bsc', glow, w_gk_up[l]) + b_gk[l]).astype(jnp.float32)
        ) / GLA_GATE_NORMALIZER
        gla_out = gla_mixer(
            gq.reshape(B, S, GLA_HEADS, GLA_DK),
            gk.reshape(B, S, GLA_HEADS, GLA_DK),
            gv.reshape(B, S, GLA_HEADS, GLA_DV),
            log_a.reshape(B, S, GLA_HEADS, GLA_DK),
            gog.reshape(B, S, GLA_HEADS, GLA_DV),
            g_gla_head[l])
        lam = (jnp.exp(jnp.sum(lambda_q1[l].astype(jnp.float32) * lambda_k1[l].astype(jnp.float32)))
               - jnp.exp(jnp.sum(lambda_q2[l].astype(jnp.float32) * lambda_k2[l].astype(jnp.float32)))
               + lambda_init)
        diff_out = diff_attention(
            dq.reshape(B, S, DIFF_HEADS, 2, DIFF_DK),
            dk.reshape(B, S, DIFF_HEADS, 2, DIFF_DK),
            dv.reshape(B, S, DIFF_HEADS, DIFF_DV),
            lam, rel_bias, g_diff_head[l], lambda_init)
        mixed = jnp.concatenate([gla_out, diff_out], axis=-1)
        mix = jnp.einsum('bsc,cd->bsd', mixed, w_out[l])
        x = x + rms_norm(mix, post_mix_g[l])
        h = rms_norm(x, pre_ffn_g[l])
        f = jax.nn.silu(jnp.einsum('bsd,df->bsf', h, w_gate[l])) * jnp.einsum('bsd,df->bsf', h, w_up[l])
        f = jnp.einsum('bsf,fd->bsd', f, w_down[l])
        x = x + rms_norm(f, post_ffn_g[l])
    return x
```

```python
import functools
import math

import jax
import jax.numpy as jnp
from jax import lax
from jax.experimental import pallas as pl
from jax.experimental.pallas import tpu as pltpu

GLA_HEADS = 4
GLA_DK = 64
GLA_DV = 128
GLA_GATE_RANK = 16
GLA_GATE_NORMALIZER = 16.0
GLA_CHUNK = 64
DIFF_HEADS = 4
DIFF_DK = 64
DIFF_DV = 128
REL_BUCKETS = 32
REL_MAX_DIST = 128
EPS = 1e-6

LANES = 128
LOG2E = 1.4426950408889634
NEG_BIG = -1e30
VMEM_LIMIT_BYTES = 56 * 1024 * 1024

TM_PROJ = 512
T_ATTN = 256

f32 = jnp.float32
bf16 = jnp.bfloat16


def _rms(x, g):
    return x * lax.rsqrt(jnp.mean(x * x, axis=-1, keepdims=True) + EPS) * g


def _inproj_kernel(x_ref, g_ref, wg_ref, wl_ref, wd_ref, wup_ref, bgk_ref,
                   gq_ref, gk_ref, gv_ref, gog_ref, loga_ref,
                   dq_ref, dk_ref, dv_ref):
    h = _rms(x_ref[...], g_ref[...]).astype(bf16)
    gla = jnp.dot(h, wg_ref[...], preferred_element_type=f32)
    nq = GLA_HEADS * GLA_DK
    nv = GLA_HEADS * GLA_DV
    gq_ref[...] = gla[:, :nq].astype(bf16)
    gk_ref[...] = gla[:, nq:2 * nq].astype(bf16)
    gv_ref[...] = gla[:, 2 * nq:2 * nq + nv].astype(bf16)
    gog_ref[...] = gla[:, 2 * nq + nv:].astype(bf16)
    low = jnp.dot(h, wl_ref[...], preferred_element_type=f32)
    z = jnp.dot(low.astype(bf16), wup_ref[...], preferred_element_type=f32) + bgk_ref[...]
    log_sig = jnp.minimum(z, 0.0) - jnp.log(1.0 + jnp.exp(-jnp.abs(z)))
    loga_ref[...] = log_sig * (1.0 / GLA_GATE_NORMALIZER)
    d = jnp.dot(h, wd_ref[...], preferred_element_type=f32)
    nd = DIFF_HEADS * 2 * DIFF_DK
    dq_ref[...] = (d[:, :nd] * (DIFF_DK ** -0.5 * LOG2E)).astype(bf16)
    dk_ref[...] = d[:, nd:2 * nd].astype(bf16)
    dv_ref[...] = d[:, 2 * nd:].astype(bf16)


def _inproj(x2, g, wg, wl, wd, wup, bgk):
    m, d = x2.shape
    tm = TM_PROJ
    nq = GLA_HEADS * GLA_DK
    nv = GLA_HEADS * GLA_DV
    nd = DIFF_HEADS * 2 * DIFF_DK
    ndv = DIFF_HEADS * DIFF_DV

    def row(n):
        return pl.BlockSpec((tm, n), lambda i: (i, 0))

    def whole(a):
        return pl.BlockSpec(a.shape, lambda i: (0, 0))

    out_shapes = [
        jax.ShapeDtypeStruct((m, nq), bf16), jax.ShapeDtypeStruct((m, nq), bf16),
        jax.ShapeDtypeStruct((m, nv), bf16), jax.ShapeDtypeStruct((m, nv), bf16),
        jax.ShapeDtypeStruct((m, nq), f32),
        jax.ShapeDtypeStruct((m, nd), bf16), jax.ShapeDtypeStruct((m, nd), bf16),
        jax.ShapeDtypeStruct((m, ndv), bf16),
    ]
    return pl.pallas_call(
        _inproj_kernel,
        grid=(m // tm,),
        in_specs=[row(d), whole(g), whole(wg), whole(wl), whole(wd), whole(wup), whole(bgk)],
        out_specs=[row(nq), row(nq), row(nv), row(nv), row(nq), row(nd), row(nd), row(ndv)],
        out_shape=out_shapes,
        compiler_params=pltpu.CompilerParams(
            dimension_semantics=("parallel",), vmem_limit_bytes=VMEM_LIMIT_BYTES),
        name="inproj",
    )(x2, g, wg, wl, wd, wup, bgk)


def _gla_kernel(q_ref, k_ref, v_ref, og_ref, la_ref, g_ref, o_ref, *, seq):
    c = GLA_CHUNK
    row_i = lax.broadcasted_iota(jnp.int32, (c, c), 0)
    col_i = lax.broadcasted_iota(jnp.int32, (c, c), 1)
    tril = row_i >= col_i
    tril_ones = jnp.where(tril, 1.0, 0.0).astype(bf16)
    lane = lax.broadcasted_iota(jnp.int32, (c, LANES), 1)
    head0_lanes = lane < GLA_DK
    srow = lax.broadcasted_iota(jnp.int32, (2 * GLA_DV, LANES), 0)
    slane = lax.broadcasted_iota(jnp.int32, (2 * GLA_DV, LANES), 1)
    state_mask = (srow < GLA_DV) == (slane < GLA_DK)
    g = g_ref[...]

    def chunk(n, state_t):
        r0 = pl.multiple_of(n * c, c)
        rows = pl.ds(r0, c)
        la = la_ref[rows, :]
        la_hi = la.astype(bf16)
        la_lo = (la - la_hi.astype(f32)).astype(bf16)
        b = (jnp.dot(tril_ones, la_hi, preferred_element_type=f32)
             + jnp.dot(tril_ones, la_lo, preferred_element_type=f32))
        b_last = b[c - 1:c, :]
        b_mid = b[c // 2 - 1:c // 2, :]
        q = q_ref[rows, :].astype(f32) * (GLA_DK ** -0.5)
        k = k_ref[rows, :].astype(f32)
        v = v_ref[rows, :]
        q_intra = (q * jnp.exp(b - b_mid)).astype(bf16)
        k_intra = (k * jnp.exp(b_mid - b)).astype(bf16)
        k_state = (k * jnp.exp(b_last - b)).astype(bf16)
        q_inter = (q * jnp.exp(b)).astype(bf16)
        zero = jnp.zeros_like(q_intra)
        q_heads = jnp.concatenate([jnp.where(head0_lanes, q_intra, zero),
                                   jnp.where(head0_lanes, zero, q_intra)], axis=0)
        scores = lax.dot_general(q_heads, k_intra, (((1,), (1,)), ((), ())),
                                 preferred_element_type=f32)
        o = lax.dot_general(q_inter, state_t.astype(bf16), (((1,), (1,)), ((), ())),
                            preferred_element_type=f32)
        outs = []
        for hh in range(2):
            s_h = jnp.where(tril, scores[hh * c:(hh + 1) * c, :], 0.0).astype(bf16)
            v_h = v[:, hh * GLA_DV:(hh + 1) * GLA_DV]
            o_h = o[:, hh * GLA_DV:(hh + 1) * GLA_DV] + jnp.dot(
                s_h, v_h, preferred_element_type=f32)
            og_h = og_ref[rows, hh * GLA_DV:(hh + 1) * GLA_DV].astype(f32)
            silu = og_h / (1.0 + jnp.exp(-og_h))
            outs.append((_rms(o_h, g) * silu).astype(bf16))
        o_ref[rows, :] = jnp.concatenate(outs, axis=1)
        u_t = lax.dot_general(v, k_state, (((0,), (0,)), ((), ())),
                              preferred_element_type=f32)
        return state_t * jnp.exp(b_last) + jnp.where(state_mask, u_t, 0.0)

    state0 = jnp.zeros((2 * GLA_DV, LANES), f32)
    lax.fori_loop(0, seq // c, chunk, state0)


def _gla(gq, gk, gv, gog, loga, g_head, batch, seq):
    m = gq.shape[0]
    pairs = GLA_HEADS // 2
    kern = functools.partial(_gla_kernel, seq=seq)
    return pl.pallas_call(
        kern,
        grid=(batch, pairs),
        in_specs=[
            pl.BlockSpec((seq, LANES), lambda b, p: (b, p)),
            pl.BlockSpec((seq, LANES), lambda b, p: (b, p)),
            pl.BlockSpec((seq, 2 * GLA_DV), lambda b, p: (b, p)),
            pl.BlockSpec((seq, 2 * GLA_DV), lambda b, p: (b, p)),
            pl.BlockSpec((seq, LANES), lambda b, p: (b, p)),
            pl.BlockSpec((1, GLA_DV), lambda b, p: (0, 0)),
        ],
        out_specs=pl.BlockSpec((seq, 2 * GLA_DV), lambda b, p: (b, p)),
        out_shape=jax.ShapeDtypeStruct((m, GLA_HEADS * GLA_DV), bf16),
        compiler_params=pltpu.CompilerParams(
            dimension_semantics=("parallel", "parallel"), vmem_limit_bytes=VMEM_LIMIT_BYTES),
        name="gla",
    )(gq, gk, gv, gog, loga, g_head)


def _diff_kernel(q_ref, k_ref, v_ref, bias_ref, lq1_ref, lk1_ref, lq2_ref, lk2_ref, g_ref,
                 o_ref, m_ref, l_ref, acc_ref, *, lambda_init):
    t = T_ATTN
    i = pl.program_id(2)
    q = q_ref[...]
    lane = lax.broadcasted_iota(jnp.int32, (t, LANES), 1)
    zero = jnp.zeros_like(q)
    qs = jnp.concatenate([jnp.where(lane < DIFF_DK, q, zero),
                          jnp.where(lane < DIFF_DK, zero, q)], axis=0)
    m_ref[...] = jnp.full(m_ref.shape, NEG_BIG, f32)
    l_ref[...] = jnp.zeros(l_ref.shape, f32)
    acc_ref[...] = jnp.zeros(acc_ref.shape, f32)

    def tile(j, bias):
        r0 = pl.multiple_of(j * t, t)
        k = k_ref[pl.ds(r0, t), :]
        v = v_ref[pl.ds(r0, t), :]
        s = lax.dot_general(qs, k, (((1,), (1,)), ((), ())),
                            preferred_element_type=f32)
        if bias is not None:
            s = s + jnp.concatenate([bias, bias], axis=0)
        m_old = m_ref[...]
        m_new = jnp.maximum(m_old, jnp.max(s, axis=-1, keepdims=True))
        alpha = jnp.exp2(m_old - m_new)
        p = jnp.exp2(s - m_new)
        l_ref[...] = alpha * l_ref[...] + jnp.sum(p, axis=-1, keepdims=True)
        acc_ref[...] = alpha * acc_ref[...] + jnp.dot(p.astype(bf16), v,
                                                      preferred_element_type=f32)
        m_ref[...] = m_new

    def far(j, carry):
        tile(j, None)
        return carry

    lax.fori_loop(0, i - 1, far, 0)

    @pl.when(i >= 1)
    def _():
        tile(i - 1, bias_ref[0, 1])

    tile(i, bias_ref[0, 0])

    lam = (jnp.exp(jnp.sum(lq1_ref[...] * lk1_ref[...], axis=-1, keepdims=True))
           - jnp.exp(jnp.sum(lq2_ref[...] * lk2_ref[...], axis=-1, keepdims=True))
           + lambda_init)
    o = acc_ref[...] / l_ref[...]
    o = o[:t, :] - lam * o[t:, :]
    o_ref[...] = (_rms(o, g_ref[...]) * (1.0 - lambda_init)).astype(bf16)


def _diff_attn(dq, dk, dv, bias_tiles, lq1, lk1, lq2, lk2, g_head, batch, seq, lambda_init):
    m = dq.shape[0]
    t = T_ATTN
    nq = seq // t
    kern = functools.partial(_diff_kernel, lambda_init=lambda_init)
    vec = pl.BlockSpec((1, DIFF_DK), lambda b, h, i: (0, 0))
    return pl.pallas_call(
        kern,
        grid=(batch, DIFF_HEADS, nq),
        in_specs=[
            pl.BlockSpec((t, LANES), lambda b, h, i: (b * nq + i, h)),
            pl.BlockSpec((seq, LANES), lambda b, h, i: (b, h)),
            pl.BlockSpec((seq, DIFF_DV), lambda b, h, i: (b, h)),
            pl.BlockSpec((1, 2, t, t), lambda b, h, i: (h, 0, 0, 0)),
            vec, vec, vec, vec,
            pl.BlockSpec((1, DIFF_DV), lambda b, h, i: (0, 0)),
        ],
        out_specs=pl.BlockSpec((t, DIFF_DV), lambda b, h, i: (b * nq + i, h)),
        out_shape=jax.ShapeDtypeStruct((m, DIFF_HEADS * DIFF_DV), bf16),
        scratch_shapes=[pltpu.VMEM((2 * t, 1), f32), pltpu.VMEM((2 * t, 1), f32),
                        pltpu.VMEM((2 * t, DIFF_DV), f32)],
        compiler_params=pltpu.CompilerParams(
            dimension_semantics=("parallel", "parallel", "arbitrary"),
            vmem_limit_bytes=VMEM_LIMIT_BYTES),
        name="diff_attn",
    )(dq, dk, dv, bias_tiles, lq1, lk1, lq2, lk2, g_head)


def _t5_bias_tiles(rel_bias, t):
    rel = jnp.arange(2 * t)
    max_exact = REL_BUCKETS // 2
    nf = jnp.maximum(rel, 1).astype(f32)
    large = max_exact + (jnp.log(nf / max_exact) / math.log(REL_MAX_DIST / max_exact)
                         * (REL_BUCKETS - max_exact)).astype(jnp.int32)
    large = jnp.minimum(large, REL_BUCKETS - 1)
    bucket = jnp.where(rel < max_exact, rel, large)
    table = rel_bias.astype(f32)
    bvec = (table[bucket] - table[REL_BUCKETS - 1][None, :]) * LOG2E
    qi = jnp.arange(t)[:, None]
    kj = jnp.arange(t)[None, :]
    r0 = qi - kj
    t0 = jnp.where((r0 >= 0)[..., None], bvec[jnp.maximum(r0, 0)], NEG_BIG)
    t1 = bvec[t + r0]
    return jnp.stack([t0, t1]).transpose(3, 0, 1, 2)


def _outffn_kernel(x_ref, gla_ref, diff_ref, wo_g_ref, wo_d_ref, post_mix_ref, pre_ffn_ref,
                   wgate_ref, wup_ref, wdown_ref, post_ffn_ref, o_ref):
    mix = (jnp.dot(gla_ref[...], wo_g_ref[...], preferred_element_type=f32)
           + jnp.dot(diff_ref[...], wo_d_ref[...], preferred_element_type=f32))
    x1 = x_ref[...] + _rms(mix, post_mix_ref[...])
    h = _rms(x1, pre_ffn_ref[...]).astype(bf16)
    gate = jnp.dot(h, wgate_ref[...], preferred_element_type=f32)
    up = jnp.dot(h, wup_ref[...], preferred_element_type=f32)
    f = (gate / (1.0 + jnp.exp(-gate)) * up).astype(bf16)
    y = jnp.dot(f, wdown_ref[...], preferred_element_type=f32)
    o_ref[...] = x1 + _rms(y, post_ffn_ref[...])


def _outffn(x2, gla_o, diff_o, wo_g, wo_d, post_mix, pre_ffn, wgate, wup, wdown, post_ffn):
    m, d = x2.shape
    tm = TM_PROJ

    def row(n):
        return pl.BlockSpec((tm, n), lambda i: (i, 0))

    def whole(a):
        return pl.BlockSpec(a.shape, lambda i: (0, 0), pipeline_mode=pl.Buffered(1))

    return pl.pallas_call(
        _outffn_kernel,
        grid=(m // tm,),
        in_specs=[row(d), row(gla_o.shape[1]), row(diff_o.shape[1]),
                  whole(wo_g), whole(wo_d), whole(post_mix), whole(pre_ffn),
                  whole(wgate), whole(wup), whole(wdown), whole(post_ffn)],
        out_specs=row(d),
        out_shape=jax.ShapeDtypeStruct((m, d), f32),
        compiler_params=pltpu.CompilerParams(
            dimension_semantics=("parallel",), vmem_limit_bytes=VMEM_LIMIT_BYTES),
        name="outffn",
    )(x2, gla_o, diff_o, wo_g, wo_d, post_mix, pre_ffn, wgate, wup, wdown, post_ffn)


def kernel(x, w_in, w_gk_up, b_gk, lambda_q1, lambda_k1, lambda_q2, lambda_k2, rel_bias,
           g_gla_head, g_diff_head, w_out, pre_mix_g, post_mix_g, w_gate, w_up, w_down,
           pre_ffn_g, post_ffn_g):
    batch, seq, d = x.shape
    depth = w_in.shape[0]
    assert seq % T_ATTN == 0 and (batch * seq) % TM_PROJ == 0 and T_ATTN >= REL_MAX_DIST
    n_gla = 2 * GLA_HEADS * GLA_DK + 2 * GLA_HEADS * GLA_DV
    bias_tiles = _t5_bias_tiles(rel_bias, T_ATTN)
    x2 = x.reshape(batch * seq, d)
    for l in range(depth):
        lambda_init = 0.8 - 0.6 * math.exp(-0.3 * l)
        wg = w_in[l, :, :n_gla].astype(bf16)
        wl = jnp.pad(w_in[l, :, n_gla:n_gla + GLA_GATE_RANK],
                     ((0, 0), (0, LANES - GLA_GATE_RANK))).astype(bf16)
        wd = w_in[l, :, n_gla + GLA_GATE_RANK:].astype(bf16)
        wup = jnp.pad(w_gk_up[l], ((0, LANES - GLA_GATE_RANK), (0, 0))).astype(bf16)
        gq, gk, gv, gog, loga, dq, dk, dv = _inproj(
            x2, pre_mix_g[l][None, :], wg, wl, wd, wup, b_gk[l][None, :])
        gla_o = _gla(gq, gk, gv, gog, loga, g_gla_head[l][None, :], batch, seq)
        diff_o = _diff_attn(dq, dk, dv, bias_tiles,
                            lambda_q1[l][None, :], lambda_k1[l][None, :],
                            lambda_q2[l][None, :], lambda_k2[l][None, :],
                            g_diff_head[l][None, :], batch, seq, lambda_init)
        n_go = GLA_HEADS * GLA_DV
        x2 = _outffn(x2, gla_o, diff_o,
                     w_out[l, :n_go].astype(bf16), w_out[l, n_go:].astype(bf16),
                     post_mix_g[l][None, :], pre_ffn_g[l][None, :],
                     w_gate[l].astype(bf16), w_up[l].astype(bf16), w_down[l].astype(bf16),
                     post_ffn_g[l][None, :])
    return x2.reshape(batch, seq, d)
```

```python
import functools
import math

import jax
import jax.numpy as jnp
from jax import lax
from jax.experimental import pallas as pl
from jax.experimental.pallas import tpu as pltpu

GLA_HEADS = 4
GLA_DK = 64
GLA_DV = 128
GLA_GATE_RANK = 16
GLA_GATE_NORMALIZER = 16.0
GLA_CHUNK = 64
DIFF_HEADS = 4
DIFF_DK = 64
DIFF_DV = 128
REL_BUCKETS = 32
REL_MAX_DIST = 128
EPS = 1e-6

LANES = 128
LOG2E = 1.4426950408889634
NEG_BIG = -1e30
VMEM_LIMIT_BYTES = 56 * 1024 * 1024

TM_PROJ = 512
T_ATTN = 256
GLA_UNROLL = 4

f32 = jnp.float32
bf16 = jnp.bfloat16


def _rms(x, g):
    return x * lax.rsqrt(jnp.mean(x * x, axis=-1, keepdims=True) + EPS) * g


def _inproj_kernel(x_ref, g_ref, wg_ref, wl_ref, wd_ref, wup_ref, bgk_ref,
                   gq_ref, gk_ref, gv_ref, gog_ref, loga_ref,
                   dq_ref, dk_ref, dv_ref):
    h = _rms(x_ref[...], g_ref[...]).astype(bf16)
    gla = jnp.dot(h, wg_ref[...], preferred_element_type=f32)
    nq = GLA_HEADS * GLA_DK
    nv = GLA_HEADS * GLA_DV
    gq_ref[...] = gla[:, :nq].astype(bf16)
    gk_ref[...] = gla[:, nq:2 * nq].astype(bf16)
    gv_ref[...] = gla[:, 2 * nq:2 * nq + nv].astype(bf16)
    gog_ref[...] = gla[:, 2 * nq + nv:].astype(bf16)
    low = jnp.dot(h, wl_ref[...], preferred_element_type=f32)
    z = jnp.dot(low.astype(bf16), wup_ref[...], preferred_element_type=f32) + bgk_ref[...]
    log_sig = jnp.minimum(z, 0.0) - jnp.log(1.0 + jnp.exp(-jnp.abs(z)))
    loga_ref[...] = log_sig * (1.0 / GLA_GATE_NORMALIZER)
    d = jnp.dot(h, wd_ref[...], preferred_element_type=f32)
    nd = DIFF_HEADS * 2 * DIFF_DK
    dq_ref[...] = (d[:, :nd] * (DIFF_DK ** -0.5 * LOG2E)).astype(bf16)
    dk_ref[...] = d[:, nd:2 * nd].astype(bf16)
    dv_ref[...] = d[:, 2 * nd:].astype(bf16)


def _inproj(x2, g, wg, wl, wd, wup, bgk):
    m, d = x2.shape
    tm = TM_PROJ
    nq = GLA_HEADS * GLA_DK
    nv = GLA_HEADS * GLA_DV
    nd = DIFF_HEADS * 2 * DIFF_DK
    ndv = DIFF_HEADS * DIFF_DV

    def row(n):
        return pl.BlockSpec((tm, n), lambda i: (i, 0))

    def whole(a):
        return pl.BlockSpec(a.shape, lambda i: (0, 0))

    out_shapes = [
        jax.ShapeDtypeStruct((m, nq), bf16), jax.ShapeDtypeStruct((m, nq), bf16),
        jax.ShapeDtypeStruct((m, nv), bf16), jax.ShapeDtypeStruct((m, nv), bf16),
        jax.ShapeDtypeStruct((m, nq), f32),
        jax.ShapeDtypeStruct((m, nd), bf16), jax.ShapeDtypeStruct((m, nd), bf16),
        jax.ShapeDtypeStruct((m, ndv), bf16),
    ]
    return pl.pallas_call(
        _inproj_kernel,
        grid=(m // tm,),
        in_specs=[row(d), whole(g), whole(wg), whole(wl), whole(wd), whole(wup), whole(bgk)],
        out_specs=[row(nq), row(nq), row(nv), row(nv), row(nq), row(nd), row(nd), row(ndv)],
        out_shape=out_shapes,
        compiler_params=pltpu.CompilerParams(
            dimension_semantics=("parallel",), vmem_limit_bytes=VMEM_LIMIT_BYTES),
        name="inproj",
    )(x2, g, wg, wl, wd, wup, bgk)


def _gla_kernel(q_ref, k_ref, v_ref, og_ref, la_ref, g_ref, o_ref, *, seq):
    c = GLA_CHUNK
    row_i = lax.broadcasted_iota(jnp.int32, (c, c), 0)
    col_i = lax.broadcasted_iota(jnp.int32, (c, c), 1)
    tril = row_i >= col_i
    tril_ones = jnp.where(tril, 1.0, 0.0).astype(bf16)
    lane = lax.broadcasted_iota(jnp.int32, (c, LANES), 1)
    head0_lanes = lane < GLA_DK
    srow = lax.broadcasted_iota(jnp.int32, (2 * GLA_DV, LANES), 0)
    slane = lax.broadcasted_iota(jnp.int32, (2 * GLA_DV, LANES), 1)
    state_mask = (srow < GLA_DV) == (slane < GLA_DK)
    g = g_ref[...]

    def chunk(n, state_t):
        r0 = pl.multiple_of(n * c, c)
        rows = pl.ds(r0, c)
        la = la_ref[rows, :]
        la_hi = la.astype(bf16)
        la_lo = (la - la_hi.astype(f32)).astype(bf16)
        b = (jnp.dot(tril_ones, la_hi, preferred_element_type=f32)
             + jnp.dot(tril_ones, la_lo, preferred_element_type=f32))
        b_last = b[c - 1:c, :]
        b_mid = b[c // 2 - 1:c // 2, :]
        q = q_ref[rows, :].astype(f32) * (GLA_DK ** -0.5)
        k = k_ref[rows, :].astype(f32)
        v = v_ref[rows, :]
        q_intra = (q * jnp.exp(b - b_mid)).astype(bf16)
        k_intra = (k * jnp.exp(b_mid - b)).astype(bf16)
        k_state = (k * jnp.exp(b_last - b)).astype(bf16)
        q_inter = (q * jnp.exp(b)).astype(bf16)
        zero = jnp.zeros_like(q_intra)
        q_heads = jnp.concatenate([jnp.where(head0_lanes, q_intra, zero),
                                   jnp.where(head0_lanes, zero, q_intra)], axis=0)
        scores = lax.dot_general(q_heads, k_intra, (((1,), (1,)), ((), ())),
                                 preferred_element_type=f32)
        o = lax.dot_general(q_inter, state_t.astype(bf16), (((1,), (1,)), ((), ())),
                            preferred_element_type=f32)
        outs = []
        for hh in range(2):
            s_h = jnp.where(tril, scores[hh * c:(hh + 1) * c, :], 0.0).astype(bf16)
            v_h = v[:, hh * GLA_DV:(hh + 1) * GLA_DV]
            o_h = o[:, hh * GLA_DV:(hh + 1) * GLA_DV] + jnp.dot(
                s_h, v_h, preferred_element_type=f32)
            og_h = og_ref[rows, hh * GLA_DV:(hh + 1) * GLA_DV].astype(f32)
            silu = og_h / (1.0 + jnp.exp(-og_h))
            outs.append((_rms(o_h, g) * silu).astype(bf16))
        o_ref[rows, :] = jnp.concatenate(outs, axis=1)
        u_t = lax.dot_general(v, k_state, (((0,), (0,)), ((), ())),
                              preferred_element_type=f32)
        return state_t * jnp.exp(b_last) + jnp.where(state_mask, u_t, 0.0)

    state0 = jnp.zeros((2 * GLA_DV, LANES), f32)
    lax.fori_loop(0, seq // c, chunk, state0, unroll=GLA_UNROLL)


def _gla(gq, gk, gv, gog, loga, g_head, batch, seq):
    m = gq.shape[0]
    pairs = GLA_HEADS // 2
    kern = functools.partial(_gla_kernel, seq=seq)
    return pl.pallas_call(
        kern,
        grid=(batch, pairs),
        in_specs=[
            pl.BlockSpec((seq, LANES), lambda b, p: (b, p)),
            pl.BlockSpec((seq, LANES), lambda b, p: (b, p)),
            pl.BlockSpec((seq, 2 * GLA_DV), lambda b, p: (b, p)),
            pl.BlockSpec((seq, 2 * GLA_DV), lambda b, p: (b, p)),
            pl.BlockSpec((seq, LANES), lambda b, p: (b, p)),
            pl.BlockSpec((1, GLA_DV), lambda b, p: (0, 0)),
        ],
        out_specs=pl.BlockSpec((seq, 2 * GLA_DV), lambda b, p: (b, p)),
        out_shape=jax.ShapeDtypeStruct((m, GLA_HEADS * GLA_DV), bf16),
        compiler_params=pltpu.CompilerParams(
            dimension_semantics=("parallel", "parallel"), vmem_limit_bytes=VMEM_LIMIT_BYTES),
        name="gla",
    )(gq, gk, gv, gog, loga, g_head)


def _diff_kernel(q_ref, k_ref, v_ref, bias_ref, lq1_ref, lk1_ref, lq2_ref, lk2_ref, g_ref,
                 o_ref, vext_ref, *, lambda_init, seq):
    t = T_ATTN
    vext_ref[:, :DIFF_DV] = v_ref[...]
    vext_ref[:, DIFF_DV:] = jnp.ones((seq, DIFF_DV), bf16)
    lane = lax.broadcasted_iota(jnp.int32, (t, LANES), 1)
    map1_lanes = lane < DIFF_DK
    lam = (jnp.exp(jnp.sum(lq1_ref[...] * lk1_ref[...], axis=-1, keepdims=True))
           - jnp.exp(jnp.sum(lq2_ref[...] * lk2_ref[...], axis=-1, keepdims=True))
           + lambda_init)
    g = g_ref[...]
    nt = (((1,), (1,)), ((), ()))
    for i in range(seq // t):
        q = q_ref[i * t:(i + 1) * t, :]
        zero = jnp.zeros_like(q)
        qs = jnp.concatenate([jnp.where(map1_lanes, q, zero),
                              jnp.where(map1_lanes, zero, q)], axis=0)
        pieces = []
        if i >= 2:
            pieces.append((0, (i - 1) * t, None))
        if i >= 1:
            pieces.append(((i - 1) * t, i * t, bias_ref[0, 1]))
        pieces.append((i * t, (i + 1) * t, bias_ref[0, 0]))
        logits = []
        for lo, hi, bias in pieces:
            s = lax.dot_general(qs, k_ref[lo:hi, :], nt, preferred_element_type=f32)
            if bias is not None:
                s = s + jnp.concatenate([bias, bias], axis=0)
            logits.append(s)
        m = functools.reduce(jnp.maximum, [jnp.max(s, axis=-1, keepdims=True) for s in logits])
        pv = None
        for (lo, hi, _), s in zip(pieces, logits):
            p = jnp.exp2(s - m).astype(bf16)
            part = jnp.dot(p, vext_ref[lo:hi, :], preferred_element_type=f32)
            pv = part if pv is None else pv + part
        o = pv[:, :DIFF_DV] / pv[:, DIFF_DV:]
        o = o[:t, :] - lam * o[t:, :]
        o_ref[i * t:(i + 1) * t, :] = (_rms(o, g) * (1.0 - lambda_init)).astype(bf16)


def _diff_attn(dq, dk, dv, bias_tiles, lq1, lk1, lq2, lk2, g_head, batch, seq, lambda_init):
    m = dq.shape[0]
    t = T_ATTN
    kern = functools.partial(_diff_kernel, lambda_init=lambda_init, seq=seq)
    vec = pl.BlockSpec((1, DIFF_DK), lambda b, h: (0, 0))
    head_rows = pl.BlockSpec((seq, LANES), lambda b, h: (b, h))
    return pl.pallas_call(
        kern,
        grid=(batch, DIFF_HEADS),
        in_specs=[
            head_rows, head_rows, head_rows,
            pl.BlockSpec((1, 2, t, t), lambda b, h: (h, 0, 0, 0)),
            vec, vec, vec, vec,
            pl.BlockSpec((1, DIFF_DV), lambda b, h: (0, 0)),
        ],
        out_specs=head_rows,
        out_shape=jax.ShapeDtypeStruct((m, DIFF_HEADS * DIFF_DV), bf16),
        scratch_shapes=[pltpu.VMEM((seq, 2 * DIFF_DV), bf16)],
        compiler_params=pltpu.CompilerParams(
            dimension_semantics=("parallel", "parallel"),
            vmem_limit_bytes=VMEM_LIMIT_BYTES),
        name="diff_attn",
    )(dq, dk, dv, bias_tiles, lq1, lk1, lq2, lk2, g_head)


def _toeplitz(g, t):
    w = jnp.concatenate([g[1:t + 1][::-1], g[:1], g[t + 1:][::-1]])
    skew = jnp.broadcast_to(w, (t, 2 * t)).reshape(-1)[:t * (2 * t - 1)]
    return skew.reshape(t, 2 * t - 1)[:, :t]


def _t5_bias_tiles(rel_bias, t):
    rel = jnp.arange(2 * t)
    max_exact = REL_BUCKETS // 2
    nf = jnp.maximum(rel, 1).astype(f32)
    large = max_exact + (jnp.log(nf / max_exact) / math.log(REL_MAX_DIST / max_exact)
                         * (REL_BUCKETS - max_exact)).astype(jnp.int32)
    large = jnp.minimum(large, REL_BUCKETS - 1)
    bucket = jnp.where(rel < max_exact, rel, large)
    table = rel_bias.astype(f32)
    bvec = (table[bucket] - table[REL_BUCKETS - 1][None, :]) * LOG2E
    tiles = []
    for h in range(DIFF_HEADS):
        b = bvec[:, h]
        diag = jnp.concatenate([jnp.full((t,), NEG_BIG, f32), b[:t]])
        tiles.append(jnp.stack([_toeplitz(diag, t), _toeplitz(b, t)]))
    return jnp.stack(tiles)


def _outffn_kernel(x_ref, gla_ref, diff_ref, wo_g_ref, wo_d_ref, post_mix_ref, pre_ffn_ref,
                   wgate_ref, wup_ref, wdown_ref, post_ffn_ref, o_ref):
    mix = (jnp.dot(gla_ref[...], wo_g_ref[...], preferred_element_type=f32)
           + jnp.dot(diff_ref[...], wo_d_ref[...], preferred_element_type=f32))
    x1 = x_ref[...] + _rms(mix, post_mix_ref[...])
    h = _rms(x1, pre_ffn_ref[...]).astype(bf16)
    gate = jnp.dot(h, wgate_ref[...], preferred_element_type=f32)
    up = jnp.dot(h, wup_ref[...], preferred_element_type=f32)
    f = (gate / (1.0 + jnp.exp(-gate)) * up).astype(bf16)
    y = jnp.dot(f, wdown_ref[...], preferred_element_type=f32)
    o_ref[...] = x1 + _rms(y, post_ffn_ref[...])


def _outffn(x2, gla_o, diff_o, wo_g, wo_d, post_mix, pre_ffn, wgate, wup, wdown, post_ffn):
    m, d = x2.shape
    tm = TM_PROJ

    def row(n):
        return pl.BlockSpec((tm, n), lambda i: (i, 0))

    def whole(a):
        return pl.BlockSpec(a.shape, lambda i: (0, 0), pipeline_mode=pl.Buffered(1))

    return pl.pallas_call(
        _outffn_kernel,
        grid=(m // tm,),
        in_specs=[row(d), row(gla_o.shape[1]), row(diff_o.shape[1]),
                  whole(wo_g), whole(wo_d), whole(post_mix), whole(pre_ffn),
                  whole(wgate), whole(wup), whole(wdown), whole(post_ffn)],
        out_specs=row(d),
        out_shape=jax.ShapeDtypeStruct((m, d), f32),
        compiler_params=pltpu.CompilerParams(
            dimension_semantics=("parallel",), vmem_limit_bytes=VMEM_LIMIT_BYTES),
        name="outffn",
    )(x2, gla_o, diff_o, wo_g, wo_d, post_mix, pre_ffn, wgate, wup, wdown, post_ffn)


def kernel(x, w_in, w_gk_up, b_gk, lambda_q1, lambda_k1, lambda_q2, lambda_k2, rel_bias,
           g_gla_head, g_diff_head, w_out, pre_mix_g, post_mix_g, w_gate, w_up, w_down,
           pre_ffn_g, post_ffn_g):
    batch, seq, d = x.shape
    depth = w_in.shape[0]
    assert seq % T_ATTN == 0 and (batch * seq) % TM_PROJ == 0 and T_ATTN >= REL_MAX_DIST
    n_gla = 2 * GLA_HEADS * GLA_DK + 2 * GLA_HEADS * GLA_DV
    bias_tiles = _t5_bias_tiles(rel_bias, T_ATTN)
    x2 = x.reshape(batch * seq, d)
    for l in range(depth):
        lambda_init = 0.8 - 0.6 * math.exp(-0.3 * l)
        wg = w_in[l, :, :n_gla].astype(bf16)
        wl = jnp.pad(w_in[l, :, n_gla:n_gla + GLA_GATE_RANK],
                     ((0, 0), (0, LANES - GLA_GATE_RANK))).astype(bf16)
        wd = w_in[l, :, n_gla + GLA_GATE_RANK:].astype(bf16)
        wup = jnp.pad(w_gk_up[l], ((0, LANES - GLA_GATE_RANK), (0, 0))).astype(bf16)
        gq, gk, gv, gog, loga, dq, dk, dv = _inproj(
            x2, pre_mix_g[l][None, :], wg, wl, wd, wup, b_gk[l][None, :])
        gla_o = _gla(gq, gk, gv, gog, loga, g_gla_head[l][None, :], batch, seq)
        diff_o = _diff_attn(dq, dk, dv, bias_tiles,
                            lambda_q1[l][None, :], lambda_k1[l][None, :],
                            lambda_q2[l][None, :], lambda_k2[l][None, :],
                            g_diff_head[l][None, :], batch, seq, lambda_init)
        n_go = GLA_HEADS * GLA_DV
        x2 = _outffn(x2, gla_o, diff_o,
                     w_out[l, :n_go].astype(bf16), w_out[l, n_go:].astype(bf16),
                     post_mix_g[l][None, :], pre_ffn_g[l][None, :],
                     w_gate[l].astype(bf16), w_up[l].astype(bf16), w_down[l].astype(bf16),
                     post_ffn_g[l][None, :])
    return x2.reshape(batch, seq, d)
```

```python
import functools
import math

import jax
import jax.numpy as jnp
from jax import lax
from jax.experimental import pallas as pl
from jax.experimental.pallas import tpu as pltpu

GLA_HEADS = 4
GLA_DK = 64
GLA_DV = 128
GLA_GATE_RANK = 16
GLA_GATE_NORMALIZER = 16.0
GLA_CHUNK = 64
DIFF_HEADS = 4
DIFF_DK = 64
DIFF_DV = 128
REL_BUCKETS = 32
REL_MAX_DIST = 128
EPS = 1e-6

LANES = 128
LOG2E = 1.4426950408889634
NEG_BIG = -1e30
VMEM_LIMIT_BYTES = 56 * 1024 * 1024

TM_PROJ = 512
T_ATTN = 256
GLA_BLOCK_CHUNKS = 8

f32 = jnp.float32
bf16 = jnp.bfloat16


def _rms(x, g):
    return x * lax.rsqrt(jnp.mean(x * x, axis=-1, keepdims=True) + EPS) * g


def _inproj_kernel(x_ref, g_ref, wg_ref, wl_ref, wd_ref, wup_ref, bgk_ref,
                   gq_ref, gk_ref, gv_ref, gog_ref, loga_ref,
                   dq_ref, dk_ref, dv_ref):
    h = _rms(x_ref[...], g_ref[...]).astype(bf16)
    gla = jnp.dot(h, wg_ref[...], preferred_element_type=f32)
    nq = GLA_HEADS * GLA_DK
    nv = GLA_HEADS * GLA_DV
    gq_ref[...] = gla[:, :nq].astype(bf16)
    gk_ref[...] = gla[:, nq:2 * nq].astype(bf16)
    gv_ref[...] = gla[:, 2 * nq:2 * nq + nv].astype(bf16)
    gog_ref[...] = gla[:, 2 * nq + nv:].astype(bf16)
    low = jnp.dot(h, wl_ref[...], preferred_element_type=f32)
    z = jnp.dot(low.astype(bf16), wup_ref[...], preferred_element_type=f32) + bgk_ref[...]
    log_sig = jnp.minimum(z, 0.0) - jnp.log(1.0 + jnp.exp(-jnp.abs(z)))
    loga_ref[...] = log_sig * (1.0 / GLA_GATE_NORMALIZER)
    d = jnp.dot(h, wd_ref[...], preferred_element_type=f32)
    nd = DIFF_HEADS * 2 * DIFF_DK
    dq_ref[...] = (d[:, :nd] * (DIFF_DK ** -0.5 * LOG2E)).astype(bf16)
    dk_ref[...] = d[:, nd:2 * nd].astype(bf16)
    dv_ref[...] = d[:, 2 * nd:].astype(bf16)


def _inproj(x2, g, wg, wl, wd, wup, bgk):
    m, d = x2.shape
    tm = TM_PROJ
    nq = GLA_HEADS * GLA_DK
    nv = GLA_HEADS * GLA_DV
    nd = DIFF_HEADS * 2 * DIFF_DK
    ndv = DIFF_HEADS * DIFF_DV

    def row(n):
        return pl.BlockSpec((tm, n), lambda i: (i, 0))

    def whole(a):
        return pl.BlockSpec(a.shape, lambda i: (0, 0))

    out_shapes = [
        jax.ShapeDtypeStruct((m, nq), bf16), jax.ShapeDtypeStruct((m, nq), bf16),
        jax.ShapeDtypeStruct((m, nv), bf16), jax.ShapeDtypeStruct((m, nv), bf16),
        jax.ShapeDtypeStruct((m, nq), f32),
        jax.ShapeDtypeStruct((m, nd), bf16), jax.ShapeDtypeStruct((m, nd), bf16),
        jax.ShapeDtypeStruct((m, ndv), bf16),
    ]
    return pl.pallas_call(
        _inproj_kernel,
        grid=(m // tm,),
        in_specs=[row(d), whole(g), whole(wg), whole(wl), whole(wd), whole(wup), whole(bgk)],
        out_specs=[row(nq), row(nq), row(nv), row(nv), row(nq), row(nd), row(nd), row(ndv)],
        out_shape=out_shapes,
        compiler_params=pltpu.CompilerParams(
            dimension_semantics=("parallel",), vmem_limit_bytes=VMEM_LIMIT_BYTES),
        name="inproj",
    )(x2, g, wg, wl, wd, wup, bgk)


def _gla_kernel(q_ref, k_ref, v_ref, og_ref, la_ref, g_ref, o_ref, *, seq):
    c = GLA_CHUNK
    row_i = lax.broadcasted_iota(jnp.int32, (c, c), 0)
    col_i = lax.broadcasted_iota(jnp.int32, (c, c), 1)
    tril = row_i >= col_i
    tril_ones = jnp.where(tril, 1.0, 0.0).astype(bf16)
    lane = lax.broadcasted_iota(jnp.int32, (c, LANES), 1)
    head0_lanes = lane < GLA_DK
    srow = lax.broadcasted_iota(jnp.int32, (2 * GLA_DV, LANES), 0)
    slane = lax.broadcasted_iota(jnp.int32, (2 * GLA_DV, LANES), 1)
    state_mask = (srow < GLA_DV) == (slane < GLA_DK)
    g = g_ref[...]

    nt = (((1,), (1,)), ((), ()))
    tn = (((0,), (0,)), ((), ()))
    nb = GLA_BLOCK_CHUNKS

    def block(blk, state_t):
        base = blk * (nb * c)
        rows = [pl.ds(pl.multiple_of(base + n * c, c), c) for n in range(nb)]
        cum = []
        for n in range(nb):
            la = la_ref[rows[n], :]
            la_hi = la.astype(bf16)
            la_lo = (la - la_hi.astype(f32)).astype(bf16)
            cum.append(jnp.dot(tril_ones, la_hi, preferred_element_type=f32)
                       + jnp.dot(tril_ones, la_lo, preferred_element_type=f32))
        q_heads, k_intra, k_state, q_inter, decay = [], [], [], [], []
        for n in range(nb):
            b = cum[n]
            b_last = b[c - 1:c, :]
            b_mid = b[c // 2 - 1:c // 2, :]
            q = q_ref[rows[n], :].astype(f32) * (GLA_DK ** -0.5)
            k = k_ref[rows[n], :].astype(f32)
            qi = (q * jnp.exp(b - b_mid)).astype(bf16)
            zero = jnp.zeros_like(qi)
            q_heads.append(jnp.concatenate([jnp.where(head0_lanes, qi, zero),
                                            jnp.where(head0_lanes, zero, qi)], axis=0))
            k_intra.append((k * jnp.exp(b_mid - b)).astype(bf16))
            k_state.append((k * jnp.exp(b_last - b)).astype(bf16))
            q_inter.append((q * jnp.exp(b)).astype(bf16))
            decay.append(jnp.exp(b_last))
        scores = [lax.dot_general(q_heads[n], k_intra[n], nt, preferred_element_type=f32)
                  for n in range(nb)]
        u_t = [lax.dot_general(v_ref[rows[n], :], k_state[n], tn, preferred_element_type=f32)
               for n in range(nb)]
        states = []
        for n in range(nb):
            states.append(state_t.astype(bf16))
            state_t = state_t * decay[n] + jnp.where(state_mask, u_t[n], 0.0)
        o_inter = [lax.dot_general(q_inter[n], states[n], nt, preferred_element_type=f32)
                   for n in range(nb)]
        for n in range(nb):
            outs = []
            for hh in range(2):
                cols = slice(hh * GLA_DV, (hh + 1) * GLA_DV)
                s_h = jnp.where(tril, scores[n][hh * c:(hh + 1) * c, :], 0.0).astype(bf16)
                o_h = o_inter[n][:, cols] + jnp.dot(s_h, v_ref[rows[n], cols],
                                                    preferred_element_type=f32)
                og_h = og_ref[rows[n], cols].astype(f32)
                silu = og_h / (1.0 + jnp.exp(-og_h))
                outs.append((_rms(o_h, g) * silu).astype(bf16))
            o_ref[rows[n], :] = jnp.concatenate(outs, axis=1)
        return state_t

    state0 = jnp.zeros((2 * GLA_DV, LANES), f32)
    lax.fori_loop(0, seq // (nb * c), block, state0)


def _gla(gq, gk, gv, gog, loga, g_head, batch, seq):
    m = gq.shape[0]
    pairs = GLA_HEADS // 2
    kern = functools.partial(_gla_kernel, seq=seq)
    return pl.pallas_call(
        kern,
        grid=(batch, pairs),
        in_specs=[
            pl.BlockSpec((seq, LANES), lambda b, p: (b, p)),
            pl.BlockSpec((seq, LANES), lambda b, p: (b, p)),
            pl.BlockSpec((seq, 2 * GLA_DV), lambda b, p: (b, p)),
            pl.BlockSpec((seq, 2 * GLA_DV), lambda b, p: (b, p)),
            pl.BlockSpec((seq, LANES), lambda b, p: (b, p)),
            pl.BlockSpec((1, GLA_DV), lambda b, p: (0, 0)),
        ],
        out_specs=pl.BlockSpec((seq, 2 * GLA_DV), lambda b, p: (b, p)),
        out_shape=jax.ShapeDtypeStruct((m, GLA_HEADS * GLA_DV), bf16),
        compiler_params=pltpu.CompilerParams(
            dimension_semantics=("parallel", "parallel"), vmem_limit_bytes=VMEM_LIMIT_BYTES),
        name="gla",
    )(gq, gk, gv, gog, loga, g_head)


def _diff_kernel(q_ref, k_ref, v_ref, bias_ref, lq1_ref, lk1_ref, lq2_ref, lk2_ref, g_ref,
                 o_ref, vext_ref, *, lambda_init, seq):
    t = T_ATTN
    vext_ref[:, :DIFF_DV] = v_ref[...]
    vext_ref[:, DIFF_DV:] = jnp.ones((seq, DIFF_DV), bf16)
    lane = lax.broadcasted_iota(jnp.int32, (t, LANES), 1)
    map1_lanes = lane < DIFF_DK
    lam = (jnp.exp(jnp.sum(lq1_ref[...] * lk1_ref[...], axis=-1, keepdims=True))
           - jnp.exp(jnp.sum(lq2_ref[...] * lk2_ref[...], axis=-1, keepdims=True))
           + lambda_init)
    g = g_ref[...]
    nt = (((1,), (1,)), ((), ()))
    nq = seq // t

    def key_ranges(i):
        pieces = []
        if i >= 2:
            pieces.append((0, (i - 1) * t, None))
        if i >= 1:
            pieces.append(((i - 1) * t, i * t, 1))
        pieces.append((i * t, (i + 1) * t, 0))
        return pieces

    def qk(i):
        q = q_ref[i * t:(i + 1) * t, :]
        zero = jnp.zeros_like(q)
        qs = jnp.concatenate([jnp.where(map1_lanes, q, zero),
                              jnp.where(map1_lanes, zero, q)], axis=0)
        logits = []
        for lo, hi, bias in key_ranges(i):
            s = lax.dot_general(qs, k_ref[lo:hi, :], nt, preferred_element_type=f32)
            if bias is not None:
                tile = bias_ref[0, bias]
                s = s + jnp.concatenate([tile, tile], axis=0)
            logits.append(s)
        return logits

    logits = qk(0)
    for i in range(nq):
        nxt = qk(i + 1) if i + 1 < nq else None
        m = functools.reduce(jnp.maximum, [jnp.max(s, axis=-1, keepdims=True) for s in logits])
        pv = None
        for (lo, hi, _), s in zip(key_ranges(i), logits):
            p = jnp.exp2(s - m).astype(bf16)
            part = jnp.dot(p, vext_ref[lo:hi, :], preferred_element_type=f32)
            pv = part if pv is None else pv + part
        o = pv[:, :DIFF_DV] / pv[:, DIFF_DV:]
        o = o[:t, :] - lam * o[t:, :]
        o_ref[i * t:(i + 1) * t, :] = (_rms(o, g) * (1.0 - lambda_init)).astype(bf16)
        logits = nxt


def _diff_attn(dq, dk, dv, bias_tiles, lq1, lk1, lq2, lk2, g_head, batch, seq, lambda_init):
    m = dq.shape[0]
    t = T_ATTN
    kern = functools.partial(_diff_kernel, lambda_init=lambda_init, seq=seq)
    vec = pl.BlockSpec((1, DIFF_DK), lambda b, h: (0, 0))
    head_rows = pl.BlockSpec((seq, LANES), lambda b, h: (b, h))
    return pl.pallas_call(
        kern,
        grid=(batch, DIFF_HEADS),
        in_specs=[
            head_rows, head_rows, head_rows,
            pl.BlockSpec((1, 2, t, t), lambda b, h: (h, 0, 0, 0)),
            vec, vec, vec, vec,
            pl.BlockSpec((1, DIFF_DV), lambda b, h: (0, 0)),
        ],
        out_specs=head_rows,
        out_shape=jax.ShapeDtypeStruct((m, DIFF_HEADS * DIFF_DV), bf16),
        scratch_shapes=[pltpu.VMEM((seq, 2 * DIFF_DV), bf16)],
        compiler_params=pltpu.CompilerParams(
            dimension_semantics=("parallel", "parallel"),
            vmem_limit_bytes=VMEM_LIMIT_BYTES),
        name="diff_attn",
    )(dq, dk, dv, bias_tiles, lq1, lk1, lq2, lk2, g_head)


def _toeplitz(g, t):
    w = jnp.concatenate([g[1:t + 1][::-1], g[:1], g[t + 1:][::-1]])
    skew = jnp.broadcast_to(w, (t, 2 * t)).reshape(-1)[:t * (2 * t - 1)]
    return skew.reshape(t, 2 * t - 1)[:, :t]


def _t5_bias_tiles(rel_bias, t):
    rel = jnp.arange(2 * t)
    max_exact = REL_BUCKETS // 2
    nf = jnp.maximum(rel, 1).astype(f32)
    large = max_exact + (jnp.log(nf / max_exact) / math.log(REL_MAX_DIST / max_exact)
                         * (REL_BUCKETS - max_exact)).astype(jnp.int32)
    large = jnp.minimum(large, REL_BUCKETS - 1)
    bucket = jnp.where(rel < max_exact, rel, large)
    table = rel_bias.astype(f32)
    bvec = (table[bucket] - table[REL_BUCKETS - 1][None, :]) * LOG2E
    tiles = []
    for h in range(DIFF_HEADS):
        b = bvec[:, h]
        diag = jnp.concatenate([jnp.full((t,), NEG_BIG, f32), b[:t]])
        tiles.append(jnp.stack([_toeplitz(diag, t), _toeplitz(b, t)]))
    return jnp.stack(tiles)


def _outffn_kernel(x_ref, gla_ref, diff_ref, wo_g_ref, wo_d_ref, post_mix_ref, pre_ffn_ref,
                   wgate_ref, wup_ref, wdown_ref, post_ffn_ref, o_ref):
    mix = (jnp.dot(gla_ref[...], wo_g_ref[...], preferred_element_type=f32)
           + jnp.dot(diff_ref[...], wo_d_ref[...], preferred_element_type=f32))
    x1 = x_ref[...] + _rms(mix, post_mix_ref[...])
    h = _rms(x1, pre_ffn_ref[...]).astype(bf16)
    gate = jnp.dot(h, wgate_ref[...], preferred_element_type=f32)
    up = jnp.dot(h, wup_ref[...], preferred_element_type=f32)
    f = (gate / (1.0 + jnp.exp(-gate)) * up).astype(bf16)
    y = jnp.dot(f, wdown_ref[...], preferred_element_type=f32)
    o_ref[...] = x1 + _rms(y, post_ffn_ref[...])


def _outffn(x2, gla_o, diff_o, wo_g, wo_d, post_mix, pre_ffn, wgate, wup, wdown, post_ffn):
    m, d = x2.shape
    tm = TM_PROJ

    def row(n):
        return pl.BlockSpec((tm, n), lambda i: (i, 0))

    def whole(a):
        return pl.BlockSpec(a.shape, lambda i: (0, 0), pipeline_mode=pl.Buffered(1))

    return pl.pallas_call(
        _outffn_kernel,
        grid=(m // tm,),
        in_specs=[row(d), row(gla_o.shape[1]), row(diff_o.shape[1]),
                  whole(wo_g), whole(wo_d), whole(post_mix), whole(pre_ffn),
                  whole(wgate), whole(wup), whole(wdown), whole(post_ffn)],
        out_specs=row(d),
        out_shape=jax.ShapeDtypeStruct((m, d), f32),
        compiler_params=pltpu.CompilerParams(
            dimension_semantics=("parallel",), vmem_limit_bytes=VMEM_LIMIT_BYTES),
        name="outffn",
    )(x2, gla_o, diff_o, wo_g, wo_d, post_mix, pre_ffn, wgate, wup, wdown, post_ffn)


def kernel(x, w_in, w_gk_up, b_gk, lambda_q1, lambda_k1, lambda_q2, lambda_k2, rel_bias,
           g_gla_head, g_diff_head, w_out, pre_mix_g, post_mix_g, w_gate, w_up, w_down,
           pre_ffn_g, post_ffn_g):
    batch, seq, d = x.shape
    depth = w_in.shape[0]
    assert seq % T_ATTN == 0 and (batch * seq) % TM_PROJ == 0 and T_ATTN >= REL_MAX_DIST
    n_gla = 2 * GLA_HEADS * GLA_DK + 2 * GLA_HEADS * GLA_DV
    bias_tiles = _t5_bias_tiles(rel_bias, T_ATTN)
    x2 = x.reshape(batch * seq, d)
    for l in range(depth):
        lambda_init = 0.8 - 0.6 * math.exp(-0.3 * l)
        wg = w_in[l, :, :n_gla].astype(bf16)
        wl = jnp.pad(w_in[l, :, n_gla:n_gla + GLA_GATE_RANK],
                     ((0, 0), (0, LANES - GLA_GATE_RANK))).astype(bf16)
        wd = w_in[l, :, n_gla + GLA_GATE_RANK:].astype(bf16)
        wup = jnp.pad(w_gk_up[l], ((0, LANES - GLA_GATE_RANK), (0, 0))).astype(bf16)
        gq, gk, gv, gog, loga, dq, dk, dv = _inproj(
            x2, pre_mix_g[l][None, :], wg, wl, wd, wup, b_gk[l][None, :])
        gla_o = _gla(gq, gk, gv, gog, loga, g_gla_head[l][None, :], batch, seq)
        diff_o = _diff_attn(dq, dk, dv, bias_tiles,
                            lambda_q1[l][None, :], lambda_k1[l][None, :],
                            lambda_q2[l][None, :], lambda_k2[l][None, :],
                            g_diff_head[l][None, :], batch, seq, lambda_init)
        n_go = GLA_HEADS * GLA_DV
        x2 = _outffn(x2, gla_o, diff_o,
                     w_out[l, :n_go].astype(bf16), w_out[l, n_go:].astype(bf16),
                     post_mix_g[l][None, :], pre_ffn_g[l][None, :],
                     w_gate[l].astype(bf16), w_up[l].astype(bf16), w_down[l].astype(bf16),
                     post_ffn_g[l][None, :])
    return x2.reshape(batch, seq, d)
```

```python
import functools
import math

import jax
import jax.numpy as jnp
from jax import lax
from jax.experimental import pallas as pl
from jax.experimental.pallas import tpu as pltpu

GLA_HEADS = 4
GLA_DK = 64
GLA_DV = 128
GLA_GATE_RANK = 16
GLA_GATE_NORMALIZER = 16.0
GLA_CHUNK = 64
DIFF_HEADS = 4
DIFF_DK = 64
DIFF_DV = 128
REL_BUCKETS = 32
REL_MAX_DIST = 128
EPS = 1e-6

LANES = 128
LOG2E = 1.4426950408889634
NEG_BIG = -1e30
VMEM_LIMIT_BYTES = 56 * 1024 * 1024

TM_PROJ = 512
T_ATTN = 256
PROJ_ROW_SPLITS = (256,)
FFN_ROW_SPLITS = (256,)
GLA_BLOCK_CHUNKS = 8

f32 = jnp.float32
bf16 = jnp.bfloat16


def _rms(x, g):
    return x * lax.rsqrt(jnp.mean(x * x, axis=-1, keepdims=True) + EPS) * g


def _inproj_kernel(x_ref, g_ref, wg_ref, wl_ref, wd_ref, wup_ref, bgk_ref,
                   gq_ref, gk_ref, gv_ref, gog_ref, loga_ref,
                   dq_ref, dk_ref, dv_ref):
    nq = GLA_HEADS * GLA_DK
    nv = GLA_HEADS * GLA_DV
    nd = DIFF_HEADS * 2 * DIFF_DK
    bounds = (0,) + PROJ_ROW_SPLITS + (x_ref.shape[0],)
    rows = [slice(lo, hi) for lo, hi in zip(bounds[:-1], bounds[1:])]
    h = [_rms(x_ref[rw, :], g_ref[...]).astype(bf16) for rw in rows]
    low = [jnp.dot(ha, wl_ref[...], preferred_element_type=f32) for ha in h]
    for rw, lw in zip(rows, low):
        z = jnp.dot(lw.astype(bf16), wup_ref[...], preferred_element_type=f32) + bgk_ref[...]
        log_sig = jnp.minimum(z, 0.0) - jnp.log(1.0 + jnp.exp(-jnp.abs(z)))
        loga_ref[rw, :] = log_sig * (1.0 / GLA_GATE_NORMALIZER)
    for rw, ha in zip(rows, h):
        gla = jnp.dot(ha, wg_ref[...], preferred_element_type=f32)
        gq_ref[rw, :] = gla[:, :nq].astype(bf16)
        gk_ref[rw, :] = gla[:, nq:2 * nq].astype(bf16)
        gv_ref[rw, :] = gla[:, 2 * nq:2 * nq + nv].astype(bf16)
        gog_ref[rw, :] = gla[:, 2 * nq + nv:].astype(bf16)
    for rw, ha in zip(rows, h):
        d = jnp.dot(ha, wd_ref[...], preferred_element_type=f32)
        dq_ref[rw, :] = (d[:, :nd] * (DIFF_DK ** -0.5 * LOG2E)).astype(bf16)
        dk_ref[rw, :] = d[:, nd:2 * nd].astype(bf16)
        dv_ref[rw, :] = d[:, 2 * nd:].astype(bf16)


def _inproj(x2, g, wg, wl, wd, wup, bgk):
    m, d = x2.shape
    tm = TM_PROJ
    nq = GLA_HEADS * GLA_DK
    nv = GLA_HEADS * GLA_DV
    nd = DIFF_HEADS * 2 * DIFF_DK
    ndv = DIFF_HEADS * DIFF_DV

    def row(n):
        return pl.BlockSpec((tm, n), lambda i: (i, 0))

    def whole(a):
        return pl.BlockSpec(a.shape, lambda i: (0, 0))

    out_shapes = [
        jax.ShapeDtypeStruct((m, nq), bf16), jax.ShapeDtypeStruct((m, nq), bf16),
        jax.ShapeDtypeStruct((m, nv), bf16), jax.ShapeDtypeStruct((m, nv), bf16),
        jax.ShapeDtypeStruct((m, nq), f32),
        jax.ShapeDtypeStruct((m, nd), bf16), jax.ShapeDtypeStruct((m, nd), bf16),
        jax.ShapeDtypeStruct((m, ndv), bf16),
    ]
    return pl.pallas_call(
        _inproj_kernel,
        grid=(m // tm,),
        in_specs=[row(d), whole(g), whole(wg), whole(wl), whole(wd), whole(wup), whole(bgk)],
        out_specs=[row(nq), row(nq), row(nv), row(nv), row(nq), row(nd), row(nd), row(ndv)],
        out_shape=out_shapes,
        compiler_params=pltpu.CompilerParams(
            dimension_semantics=("parallel",), vmem_limit_bytes=VMEM_LIMIT_BYTES),
        name="inproj",
    )(x2, g, wg, wl, wd, wup, bgk)


def _gla_kernel(q_ref, k_ref, v_ref, og_ref, la_ref, g_ref, o_ref, *, seq):
    c = GLA_CHUNK
    row_i = lax.broadcasted_iota(jnp.int32, (c, c), 0)
    col_i = lax.broadcasted_iota(jnp.int32, (c, c), 1)
    tril = row_i >= col_i
    tril_ones = jnp.where(tril, 1.0, 0.0).astype(bf16)
    lane = lax.broadcasted_iota(jnp.int32, (c, LANES), 1)
    head0_lanes = lane < GLA_DK
    srow = lax.broadcasted_iota(jnp.int32, (2 * GLA_DV, LANES), 0)
    slane = lax.broadcasted_iota(jnp.int32, (2 * GLA_DV, LANES), 1)
    state_mask = (srow < GLA_DV) == (slane < GLA_DK)
    g = g_ref[...]

    nt = (((1,), (1,)), ((), ()))
    tn = (((0,), (0,)), ((), ()))
    nb = GLA_BLOCK_CHUNKS

    def block(blk, state_t):
        base = blk * (nb * c)
        rows = [pl.ds(pl.multiple_of(base + n * c, c), c) for n in range(nb)]
        cum = []
        for n in range(nb):
            la = la_ref[rows[n], :]
            la_hi = la.astype(bf16)
            la_lo = (la - la_hi.astype(f32)).astype(bf16)
            cum.append(jnp.dot(tril_ones, la_hi, preferred_element_type=f32)
                       + jnp.dot(tril_ones, la_lo, preferred_element_type=f32))
        q_heads, k_intra, k_state, q_inter, decay = [], [], [], [], []
        for n in range(nb):
            b = cum[n]
            b_last = b[c - 1:c, :]
            b_mid = b[c // 2 - 1:c // 2, :]
            q = q_ref[rows[n], :].astype(f32) * (GLA_DK ** -0.5)
            k = k_ref[rows[n], :].astype(f32)
            qi = (q * jnp.exp(b - b_mid)).astype(bf16)
            zero = jnp.zeros_like(qi)
            q_heads.append(jnp.concatenate([jnp.where(head0_lanes, qi, zero),
                                            jnp.where(head0_lanes, zero, qi)], axis=0))
            k_intra.append((k * jnp.exp(b_mid - b)).astype(bf16))
            k_state.append((k * jnp.exp(b_last - b)).astype(bf16))
            q_inter.append((q * jnp.exp(b)).astype(bf16))
            decay.append(jnp.exp(b_last))
        scores = [lax.dot_general(q_heads[n], k_intra[n], nt, preferred_element_type=f32)
                  for n in range(nb)]
        u_t = [lax.dot_general(v_ref[rows[n], :], k_state[n], tn, preferred_element_type=f32)
               for n in range(nb)]
        states = []
        for n in range(nb):
            states.append(state_t.astype(bf16))
            state_t = state_t * decay[n] + jnp.where(state_mask, u_t[n], 0.0)
        o_inter = [lax.dot_general(q_inter[n], states[n], nt, preferred_element_type=f32)
                   for n in range(nb)]
        for n in range(nb):
            outs = []
            for hh in range(2):
                cols = slice(hh * GLA_DV, (hh + 1) * GLA_DV)
                s_h = jnp.where(tril, scores[n][hh * c:(hh + 1) * c, :], 0.0).astype(bf16)
                o_h = o_inter[n][:, cols] + jnp.dot(s_h, v_ref[rows[n], cols],
                                                    preferred_element_type=f32)
                og_h = og_ref[rows[n], cols].astype(f32)
                silu = og_h / (1.0 + jnp.exp(-og_h))
                outs.append((_rms(o_h, g) * silu).astype(bf16))
            o_ref[rows[n], :] = jnp.concatenate(outs, axis=1)
        return state_t

    state0 = jnp.zeros((2 * GLA_DV, LANES), f32)
    lax.fori_loop(0, seq // (nb * c), block, state0)


def _gla(gq, gk, gv, gog, loga, g_head, batch, seq):
    m = gq.shape[0]
    pairs = GLA_HEADS // 2
    kern = functools.partial(_gla_kernel, seq=seq)
    return pl.pallas_call(
        kern,
        grid=(batch, pairs),
        in_specs=[
            pl.BlockSpec((seq, LANES), lambda b, p: (b, p)),
            pl.BlockSpec((seq, LANES), lambda b, p: (b, p)),
            pl.BlockSpec((seq, 2 * GLA_DV), lambda b, p: (b, p)),
            pl.BlockSpec((seq, 2 * GLA_DV), lambda b, p: (b, p)),
            pl.BlockSpec((seq, LANES), lambda b, p: (b, p)),
            pl.BlockSpec((1, GLA_DV), lambda b, p: (0, 0)),
        ],
        out_specs=pl.BlockSpec((seq, 2 * GLA_DV), lambda b, p: (b, p)),
        out_shape=jax.ShapeDtypeStruct((m, GLA_HEADS * GLA_DV), bf16),
        compiler_params=pltpu.CompilerParams(
            dimension_semantics=("parallel", "parallel"), vmem_limit_bytes=VMEM_LIMIT_BYTES),
        name="gla",
    )(gq, gk, gv, gog, loga, g_head)


def _diff_kernel(q_ref, k_ref, v_ref, bias_ref, lq1_ref, lk1_ref, lq2_ref, lk2_ref, g_ref,
                 o_ref, vext_ref, *, lambda_init, seq):
    t = T_ATTN
    vext_ref[:, :DIFF_DV] = v_ref[...]
    vext_ref[:, DIFF_DV:] = jnp.ones((seq, DIFF_DV), bf16)
    lane = lax.broadcasted_iota(jnp.int32, (t, LANES), 1)
    map1_lanes = lane < DIFF_DK
    lam = (jnp.exp(jnp.sum(lq1_ref[...] * lk1_ref[...], axis=-1, keepdims=True))
           - jnp.exp(jnp.sum(lq2_ref[...] * lk2_ref[...], axis=-1, keepdims=True))
           + lambda_init)
    g = g_ref[...]
    nt = (((1,), (1,)), ((), ()))
    nq = seq // t

    def key_ranges(i):
        pieces = []
        if i >= 2:
            pieces.append((0, (i - 1) * t, None))
        if i >= 1:
            pieces.append(((i - 1) * t, i * t, 1))
        pieces.append((i * t, (i + 1) * t, 0))
        return pieces

    def qk(i):
        q = q_ref[i * t:(i + 1) * t, :]
        zero = jnp.zeros_like(q)
        qs = jnp.concatenate([jnp.where(map1_lanes, q, zero),
                              jnp.where(map1_lanes, zero, q)], axis=0)
        logits = []
        for lo, hi, bias in key_ranges(i):
            s = lax.dot_general(qs, k_ref[lo:hi, :], nt, preferred_element_type=f32)
            if bias is not None:
                tile = bias_ref[0, bias]
                s = s + jnp.concatenate([tile, tile], axis=0)
            logits.append(s)
        return logits

    logits = qk(0)
    for i in range(nq):
        nxt = qk(i + 1) if i + 1 < nq else None
        m = functools.reduce(jnp.maximum, [jnp.max(s, axis=-1, keepdims=True) for s in logits])
        pv = None
        for (lo, hi, _), s in zip(key_ranges(i), logits):
            p = jnp.exp2(s - m).astype(bf16)
            part = jnp.dot(p, vext_ref[lo:hi, :], preferred_element_type=f32)
            pv = part if pv is None else pv + part
        o = pv[:, :DIFF_DV] / pv[:, DIFF_DV:]
        o = o[:t, :] - lam * o[t:, :]
        o_ref[i * t:(i + 1) * t, :] = (_rms(o, g) * (1.0 - lambda_init)).astype(bf16)
        logits = nxt


def _diff_attn(dq, dk, dv, bias_tiles, lq1, lk1, lq2, lk2, g_head, batch, seq, lambda_init):
    m = dq.shape[0]
    t = T_ATTN
    kern = functools.partial(_diff_kernel, lambda_init=lambda_init, seq=seq)
    vec = pl.BlockSpec((1, DIFF_DK), lambda b, h: (0, 0))
    head_rows = pl.BlockSpec((seq, LANES), lambda b, h: (b, h))
    return pl.pallas_call(
        kern,
        grid=(batch, DIFF_HEADS),
        in_specs=[
            head_rows, head_rows, head_rows,
            pl.BlockSpec((1, 2, t, t), lambda b, h: (h, 0, 0, 0)),
            vec, vec, vec, vec,
            pl.BlockSpec((1, DIFF_DV), lambda b, h: (0, 0)),
        ],
        out_specs=head_rows,
        out_shape=jax.ShapeDtypeStruct((m, DIFF_HEADS * DIFF_DV), bf16),
        scratch_shapes=[pltpu.VMEM((seq, 2 * DIFF_DV), bf16)],
        compiler_params=pltpu.CompilerParams(
            dimension_semantics=("parallel", "parallel"),
            vmem_limit_bytes=VMEM_LIMIT_BYTES),
        name="diff_attn",
    )(dq, dk, dv, bias_tiles, lq1, lk1, lq2, lk2, g_head)


def _toeplitz(g, t):
    w = jnp.concatenate([g[1:t + 1][::-1], g[:1], g[t + 1:][::-1]])
    skew = jnp.broadcast_to(w, (t, 2 * t)).reshape(-1)[:t * (2 * t - 1)]
    return skew.reshape(t, 2 * t - 1)[:, :t]


def _t5_bias_tiles(rel_bias, t):
    rel = jnp.arange(2 * t)
    max_exact = REL_BUCKETS // 2
    nf = jnp.maximum(rel, 1).astype(f32)
    large = max_exact + (jnp.log(nf / max_exact) / math.log(REL_MAX_DIST / max_exact)
                         * (REL_BUCKETS - max_exact)).astype(jnp.int32)
    large = jnp.minimum(large, REL_BUCKETS - 1)
    bucket = jnp.where(rel < max_exact, rel, large)
    table = rel_bias.astype(f32)
    bvec = (table[bucket] - table[REL_BUCKETS - 1][None, :]) * LOG2E
    tiles = []
    for h in range(DIFF_HEADS):
        b = bvec[:, h]
        diag = jnp.concatenate([jnp.full((t,), NEG_BIG, f32), b[:t]])
        tiles.append(jnp.stack([_toeplitz(diag, t), _toeplitz(b, t)]))
    return jnp.stack(tiles)


def _outffn_kernel(x_ref, gla_ref, diff_ref, wo_g_ref, wo_d_ref, post_mix_ref, pre_ffn_ref,
                   wgate_ref, wup_ref, wdown_ref, post_ffn_ref, o_ref):
    bounds = (0,) + FFN_ROW_SPLITS + (x_ref.shape[0],)
    rows = [slice(lo, hi) for lo, hi in zip(bounds[:-1], bounds[1:])]
    mix = [jnp.dot(gla_ref[rw, :], wo_g_ref[...], preferred_element_type=f32)
           + jnp.dot(diff_ref[rw, :], wo_d_ref[...], preferred_element_type=f32) for rw in rows]
    x1 = [x_ref[rw, :] + _rms(mx, post_mix_ref[...]) for rw, mx in zip(rows, mix)]
    h = [_rms(xa, pre_ffn_ref[...]).astype(bf16) for xa in x1]
    gate_up = [(jnp.dot(ha, wgate_ref[...], preferred_element_type=f32),
                jnp.dot(ha, wup_ref[...], preferred_element_type=f32)) for ha in h]
    f = [(gate / (1.0 + jnp.exp(-gate)) * up).astype(bf16) for gate, up in gate_up]
    y = [jnp.dot(fa, wdown_ref[...], preferred_element_type=f32) for fa in f]
    for rw, xa, ya in zip(rows, x1, y):
        o_ref[rw, :] = xa + _rms(ya, post_ffn_ref[...])


def _outffn(x2, gla_o, diff_o, wo_g, wo_d, post_mix, pre_ffn, wgate, wup, wdown, post_ffn):
    m, d = x2.shape
    tm = TM_PROJ

    def row(n):
        return pl.BlockSpec((tm, n), lambda i: (i, 0))

    def whole(a):
        return pl.BlockSpec(a.shape, lambda i: (0, 0), pipeline_mode=pl.Buffered(1))

    return pl.pallas_call(
        _outffn_kernel,
        grid=(m // tm,),
        in_specs=[row(d), row(gla_o.shape[1]), row(diff_o.shape[1]),
                  whole(wo_g), whole(wo_d), whole(post_mix), whole(pre_ffn),
                  whole(wgate), whole(wup), whole(wdown), whole(post_ffn)],
        out_specs=row(d),
        out_shape=jax.ShapeDtypeStruct((m, d), f32),
        compiler_params=pltpu.CompilerParams(
            dimension_semantics=("parallel",), vmem_limit_bytes=VMEM_LIMIT_BYTES),
        name="outffn",
    )(x2, gla_o, diff_o, wo_g, wo_d, post_mix, pre_ffn, wgate, wup, wdown, post_ffn)


def kernel(x, w_in, w_gk_up, b_gk, lambda_q1, lambda_k1, lambda_q2, lambda_k2, rel_bias,
           g_gla_head, g_diff_head, w_out, pre_mix_g, post_mix_g, w_gate, w_up, w_down,
           pre_ffn_g, post_ffn_g):
    batch, seq, d = x.shape
    depth = w_in.shape[0]
    assert seq % T_ATTN == 0 and (batch * seq) % TM_PROJ == 0 and T_ATTN >= REL_MAX_DIST
    n_gla = 2 * GLA_HEADS * GLA_DK + 2 * GLA_HEADS * GLA_DV
    bias_tiles = _t5_bias_tiles(rel_bias, T_ATTN)
    x2 = x.reshape(batch * seq, d)
    for l in range(depth):
        lambda_init = 0.8 - 0.6 * math.exp(-0.3 * l)
        wg = w_in[l, :, :n_gla].astype(bf16)
        wl = jnp.pad(w_in[l, :, n_gla:n_gla + GLA_GATE_RANK],
                     ((0, 0), (0, LANES - GLA_GATE_RANK))).astype(bf16)
        wd = w_in[l, :, n_gla + GLA_GATE_RANK:].astype(bf16)
        wup = jnp.pad(w_gk_up[l], ((0, LANES - GLA_GATE_RANK), (0, 0))).astype(bf16)
        gq, gk, gv, gog, loga, dq, dk, dv = _inproj(
            x2, pre_mix_g[l][None, :], wg, wl, wd, wup, b_gk[l][None, :])
        gla_o = _gla(gq, gk, gv, gog, loga, g_gla_head[l][None, :], batch, seq)
        diff_o = _diff_attn(dq, dk, dv, bias_tiles,
                            lambda_q1[l][None, :], lambda_k1[l][None, :],
                            lambda_q2[l][None, :], lambda_k2[l][None, :],
                            g_diff_head[l][None, :], batch, seq, lambda_init)
        n_go = GLA_HEADS * GLA_DV
        x2 = _outffn(x2, gla_o, diff_o,
                     w_out[l, :n_go].astype(bf16), w_out[l, n_go:].astype(bf16),
                     post_mix_g[l][None, :], pre_ffn_g[l][None, :],
                     w_gate[l].astype(bf16), w_up[l].astype(bf16), w_down[l].astype(bf16),
                     post_ffn_g[l][None, :])
    return x2.reshape(batch, seq, d)
```

```python
import functools
import math

import jax
import jax.numpy as jnp
from jax import lax
from jax.experimental import pallas as pl
from jax.experimental.pallas import tpu as pltpu

GLA_HEADS = 4
GLA_DK = 64
GLA_DV = 128
GLA_GATE_RANK = 16
GLA_GATE_NORMALIZER = 16.0
GLA_CHUNK = 64
DIFF_HEADS = 4
DIFF_DK = 64
DIFF_DV = 128
REL_BUCKETS = 32
REL_MAX_DIST = 128
EPS = 1e-6

LANES = 128
LOG2E = 1.4426950408889634
NEG_BIG = -1e30
VMEM_LIMIT_BYTES = 56 * 1024 * 1024

TM_PROJ = 512
T_ATTN = 256
QK_LOOKAHEAD = 3
ONES_ROWS = 16
PROJ_ROW_SPLITS = (256,)
FFN_ROW_SPLITS = (256,)
GLA_BLOCK_CHUNKS = 8

f32 = jnp.float32
bf16 = jnp.bfloat16


def _rms(x, g):
    return x * lax.rsqrt(jnp.mean(x * x, axis=-1, keepdims=True) + EPS) * g


def _inproj_kernel(x_ref, g_ref, wg_ref, wl_ref, wdqk_ref, wdvt_ref, wup_ref, bgk_ref,
                   gq_ref, gk_ref, gv_ref, gog_ref, loga_ref,
                   dq_ref, dk_ref, dvt_ref):
    nq = GLA_HEADS * GLA_DK
    nv = GLA_HEADS * GLA_DV
    nd = DIFF_HEADS * 2 * DIFF_DK
    bounds = (0,) + PROJ_ROW_SPLITS + (x_ref.shape[0],)
    rows = [slice(lo, hi) for lo, hi in zip(bounds[:-1], bounds[1:])]
    h = [_rms(x_ref[rw, :], g_ref[...]).astype(bf16) for rw in rows]
    low = [jnp.dot(ha, wl_ref[...], preferred_element_type=f32) for ha in h]
    for rw, lw in zip(rows, low):
        z = jnp.dot(lw.astype(bf16), wup_ref[...], preferred_element_type=f32) + bgk_ref[...]
        log_sig = jnp.minimum(z, 0.0) - jnp.log(1.0 + jnp.exp(-jnp.abs(z)))
        loga_ref[rw, :] = log_sig * (1.0 / GLA_GATE_NORMALIZER)
    for rw, ha in zip(rows, h):
        gla = jnp.dot(ha, wg_ref[...], preferred_element_type=f32)
        gq_ref[rw, :] = gla[:, :nq].astype(bf16)
        gk_ref[rw, :] = gla[:, nq:2 * nq].astype(bf16)
        gv_ref[rw, :] = gla[:, 2 * nq:2 * nq + nv].astype(bf16)
        gog_ref[rw, :] = gla[:, 2 * nq + nv:].astype(bf16)
    for rw, ha in zip(rows, h):
        d = jnp.dot(ha, wdqk_ref[...], preferred_element_type=f32)
        dq_ref[rw, :] = (d[:, :nd] * (DIFF_DK ** -0.5 * LOG2E)).astype(bf16)
        dk_ref[rw, :] = d[:, nd:].astype(bf16)
    for rw, ha in zip(rows, h):
        dvt = lax.dot_general(wdvt_ref[...], ha, (((1,), (1,)), ((), ())),
                              preferred_element_type=f32)
        dvt_ref[:, rw] = dvt.astype(bf16)


def _inproj(x2, g, wg, wl, wdqk, wdvt, wup, bgk):
    m, d = x2.shape
    tm = TM_PROJ
    nq = GLA_HEADS * GLA_DK
    nv = GLA_HEADS * GLA_DV
    nd = DIFF_HEADS * 2 * DIFF_DK
    ndv = DIFF_HEADS * DIFF_DV

    def row(n):
        return pl.BlockSpec((tm, n), lambda i: (i, 0))

    def whole(a):
        return pl.BlockSpec(a.shape, lambda i: (0, 0))

    out_shapes = [
        jax.ShapeDtypeStruct((m, nq), bf16), jax.ShapeDtypeStruct((m, nq), bf16),
        jax.ShapeDtypeStruct((m, nv), bf16), jax.ShapeDtypeStruct((m, nv), bf16),
        jax.ShapeDtypeStruct((m, nq), f32),
        jax.ShapeDtypeStruct((m, nd), bf16), jax.ShapeDtypeStruct((m, nd), bf16),
        jax.ShapeDtypeStruct((ndv, m), bf16),
    ]
    return pl.pallas_call(
        _inproj_kernel,
        grid=(m // tm,),
        in_specs=[row(d), whole(g), whole(wg), whole(wl), whole(wdqk), whole(wdvt), whole(wup),
                  whole(bgk)],
        out_specs=[row(nq), row(nq), row(nv), row(nv), row(nq), row(nd), row(nd),
                   pl.BlockSpec((ndv, tm), lambda i: (0, i))],
        out_shape=out_shapes,
        compiler_params=pltpu.CompilerParams(
            dimension_semantics=("parallel",), vmem_limit_bytes=VMEM_LIMIT_BYTES),
        name="inproj",
    )(x2, g, wg, wl, wdqk, wdvt, wup, bgk)


def _gla_kernel(q_ref, k_ref, v_ref, og_ref, la_ref, g_ref, o_ref, *, seq):
    c = GLA_CHUNK
    row_i = lax.broadcasted_iota(jnp.int32, (c, c), 0)
    col_i = lax.broadcasted_iota(jnp.int32, (c, c), 1)
    tril = row_i >= col_i
    tril_ones = jnp.where(tril, 1.0, 0.0).astype(bf16)
    lane = lax.broadcasted_iota(jnp.int32, (c, LANES), 1)
    head0_lanes = lane < GLA_DK
    srow = lax.broadcasted_iota(jnp.int32, (2 * GLA_DV, LANES), 0)
    slane = lax.broadcasted_iota(jnp.int32, (2 * GLA_DV, LANES), 1)
    state_mask = (srow < GLA_DV) == (slane < GLA_DK)
    g = g_ref[...]

    nt = (((1,), (1,)), ((), ()))
    tn = (((0,), (0,)), ((), ()))
    nb = GLA_BLOCK_CHUNKS

    def block(blk, state_t):
        base = blk * (nb * c)
        rows = [pl.ds(pl.multiple_of(base + n * c, c), c) for n in range(nb)]
        cum = []
        for n in range(nb):
            la = la_ref[rows[n], :]
            la_hi = la.astype(bf16)
            la_lo = (la - la_hi.astype(f32)).astype(bf16)
            cum.append(jnp.dot(tril_ones, la_hi, preferred_element_type=f32)
                       + jnp.dot(tril_ones, la_lo, preferred_element_type=f32))
        q_heads, k_intra, k_state, q_inter, decay = [], [], [], [], []
        for n in range(nb):
            b = cum[n]
            b_last = b[c - 1:c, :]
            b_mid = b[c // 2 - 1:c // 2, :]
            q = q_ref[rows[n], :].astype(f32) * (GLA_DK ** -0.5)
            k = k_ref[rows[n], :].astype(f32)
            qi = (q * jnp.exp(b - b_mid)).astype(bf16)
            zero = jnp.zeros_like(qi)
            q_heads.append(jnp.concatenate([jnp.where(head0_lanes, qi, zero),
                                            jnp.where(head0_lanes, zero, qi)], axis=0))
            k_intra.append((k * jnp.exp(b_mid - b)).astype(bf16))
            k_state.append((k * jnp.exp(b_last - b)).astype(bf16))
            q_inter.append((q * jnp.exp(b)).astype(bf16))
            decay.append(jnp.exp(b_last))
        scores = [lax.dot_general(q_heads[n], k_intra[n], nt, preferred_element_type=f32)
                  for n in range(nb)]
        u_t = [lax.dot_general(v_ref[rows[n], :], k_state[n], tn, preferred_element_type=f32)
               for n in range(nb)]
        states = []
        for n in range(nb):
            states.append(state_t.astype(bf16))
            state_t = state_t * decay[n] + jnp.where(state_mask, u_t[n], 0.0)
        o_inter = [lax.dot_general(q_inter[n], states[n], nt, preferred_element_type=f32)
                   for n in range(nb)]
        for n in range(nb):
            outs = []
            for hh in range(2):
                cols = slice(hh * GLA_DV, (hh + 1) * GLA_DV)
                s_h = jnp.where(tril, scores[n][hh * c:(hh + 1) * c, :], 0.0).astype(bf16)
                o_h = o_inter[n][:, cols] + jnp.dot(s_h, v_ref[rows[n], cols],
                                                    preferred_element_type=f32)
                og_h = og_ref[rows[n], cols].astype(f32)
                silu = og_h / (1.0 + jnp.exp(-og_h))
                outs.append((_rms(o_h, g) * silu).astype(bf16))
            o_ref[rows[n], :] = jnp.concatenate(outs, axis=1)
        return state_t

    state0 = jnp.zeros((2 * GLA_DV, LANES), f32)
    lax.fori_loop(0, seq // (nb * c), block, state0)


def _gla(gq, gk, gv, gog, loga, g_head, batch, seq):
    m = gq.shape[0]
    pairs = GLA_HEADS // 2
    kern = functools.partial(_gla_kernel, seq=seq)
    return pl.pallas_call(
        kern,
        grid=(batch, pairs),
        in_specs=[
            pl.BlockSpec((seq, LANES), lambda b, p: (b, p)),
            pl.BlockSpec((seq, LANES), lambda b, p: (b, p)),
            pl.BlockSpec((seq, 2 * GLA_DV), lambda b, p: (b, p)),
            pl.BlockSpec((seq, 2 * GLA_DV), lambda b, p: (b, p)),
            pl.BlockSpec((seq, LANES), lambda b, p: (b, p)),
            pl.BlockSpec((1, GLA_DV), lambda b, p: (0, 0)),
        ],
        out_specs=pl.BlockSpec((seq, 2 * GLA_DV), lambda b, p: (b, p)),
        out_shape=jax.ShapeDtypeStruct((m, GLA_HEADS * GLA_DV), bf16),
        compiler_params=pltpu.CompilerParams(
            dimension_semantics=("parallel", "parallel"), vmem_limit_bytes=VMEM_LIMIT_BYTES),
        name="gla",
    )(gq, gk, gv, gog, loga, g_head)


def _diff_kernel(q_ref, k_ref, vt_ref, bias_ref, lq1_ref, lk1_ref, lq2_ref, lk2_ref, g_ref,
                 o_ref, vext_ref, *, lambda_init, seq):
    t = T_ATTN
    vext_ref[:DIFF_DV, :] = vt_ref[...]
    vext_ref[DIFF_DV:, :] = jnp.ones((ONES_ROWS, seq), bf16)
    lane = lax.broadcasted_iota(jnp.int32, (t, LANES), 1)
    map1_lanes = lane < DIFF_DK
    lam = (jnp.exp(jnp.sum(lq1_ref[...] * lk1_ref[...], axis=-1, keepdims=True))
           - jnp.exp(jnp.sum(lq2_ref[...] * lk2_ref[...], axis=-1, keepdims=True))
           + lambda_init)
    g = g_ref[...]
    nt = (((1,), (1,)), ((), ()))
    nq = seq // t

    def key_ranges(i):
        pieces = []
        if i >= 2:
            pieces.append((0, (i - 1) * t, None))
        if i >= 1:
            pieces.append(((i - 1) * t, i * t, 1))
        pieces.append((i * t, (i + 1) * t, 0))
        return pieces

    def qk(unit):
        i, second_map = divmod(unit, 2)
        q = q_ref[i * t:(i + 1) * t, :]
        keep = jnp.logical_not(map1_lanes) if second_map else map1_lanes
        qm = jnp.where(keep, q, jnp.zeros_like(q))
        logits = []
        for lo, hi, bias in key_ranges(i):
            s = lax.dot_general(k_ref[lo:hi, :], qm, nt, preferred_element_type=f32)
            if bias is not None:
                s = s + bias_ref[0, bias]
            logits.append(s)
        return logits

    def softmax_pv(unit, logits):
        i = unit // 2
        m = functools.reduce(jnp.maximum, [jnp.max(s, axis=0, keepdims=True) for s in logits])
        pv = None
        for (lo, hi, _), s in zip(key_ranges(i), logits):
            p = jnp.exp2(s - m).astype(bf16)
            part = jnp.dot(vext_ref[:, lo:hi], p, preferred_element_type=f32)
            pv = part if pv is None else pv + part
        return pv[:DIFF_DV, :] / pv[DIFF_DV:DIFF_DV + 1, :]

    n_units = 2 * nq
    ahead = [qk(u) for u in range(min(QK_LOOKAHEAD, n_units))]
    for i in range(nq):
        maps = []
        for u in (2 * i, 2 * i + 1):
            if u + QK_LOOKAHEAD < n_units:
                ahead.append(qk(u + QK_LOOKAHEAD))
            maps.append(softmax_pv(u, ahead.pop(0)))
        o = (maps[0] - lam * maps[1]).T
        o_ref[i * t:(i + 1) * t, :] = (_rms(o, g) * (1.0 - lambda_init)).astype(bf16)


def _diff_attn(dq, dk, dvt, bias_tiles, lq1, lk1, lq2, lk2, g_head, batch, seq, lambda_init):
    m = dq.shape[0]
    t = T_ATTN
    kern = functools.partial(_diff_kernel, lambda_init=lambda_init, seq=seq)
    vec = pl.BlockSpec((1, DIFF_DK), lambda b, h: (0, 0))
    head_rows = pl.BlockSpec((seq, LANES), lambda b, h: (b, h))
    return pl.pallas_call(
        kern,
        grid=(batch, DIFF_HEADS),
        in_specs=[
            head_rows, head_rows,
            pl.BlockSpec((DIFF_DV, seq), lambda b, h: (h, b)),
            pl.BlockSpec((1, 2, t, t), lambda b, h: (h, 0, 0, 0)),
            vec, vec, vec, vec,
            pl.BlockSpec((1, DIFF_DV), lambda b, h: (0, 0)),
        ],
        out_specs=head_rows,
        out_shape=jax.ShapeDtypeStruct((m, DIFF_HEADS * DIFF_DV), bf16),
        scratch_shapes=[pltpu.VMEM((DIFF_DV + ONES_ROWS, seq), bf16)],
        compiler_params=pltpu.CompilerParams(
            dimension_semantics=("parallel", "parallel"),
            vmem_limit_bytes=VMEM_LIMIT_BYTES),
        name="diff_attn",
    )(dq, dk, dvt, bias_tiles, lq1, lk1, lq2, lk2, g_head)


def _toeplitz(g, t):
    w = jnp.concatenate([g[1:t + 1][::-1], g[:1], g[t + 1:][::-1]])
    skew = jnp.broadcast_to(w, (t, 2 * t)).reshape(-1)[:t * (2 * t - 1)]
    return skew.reshape(t, 2 * t - 1)[:, :t]


def _t5_bias_tiles(rel_bias, t):
    rel = jnp.arange(2 * t)
    max_exact = REL_BUCKETS // 2
    nf = jnp.maximum(rel, 1).astype(f32)
    large = max_exact + (jnp.log(nf / max_exact) / math.log(REL_MAX_DIST / max_exact)
                         * (REL_BUCKETS - max_exact)).astype(jnp.int32)
    large = jnp.minimum(large, REL_BUCKETS - 1)
    bucket = jnp.where(rel < max_exact, rel, large)
    table = rel_bias.astype(f32)
    bvec = (table[bucket] - table[REL_BUCKETS - 1][None, :]) * LOG2E
    tiles = []
    for h in range(DIFF_HEADS):
        b = bvec[:, h]
        diag = jnp.concatenate([jnp.full((t,), NEG_BIG, f32), b[:t]])
        pair = [_toeplitz(diag, t), _toeplitz(b, t)]
        tiles.append(jnp.stack([tile.T for tile in pair]))
    return jnp.stack(tiles)


def _outffn_kernel(x_ref, gla_ref, diff_ref, wo_g_ref, wo_d_ref, post_mix_ref, pre_ffn_ref,
                   wgate_ref, wup_ref, wdown_ref, post_ffn_ref, o_ref):
    bounds = (0,) + FFN_ROW_SPLITS + (x_ref.shape[0],)
    rows = [slice(lo, hi) for lo, hi in zip(bounds[:-1], bounds[1:])]
    mix = [jnp.dot(gla_ref[rw, :], wo_g_ref[...], preferred_element_type=f32)
           + jnp.dot(diff_ref[rw, :], wo_d_ref[...], preferred_element_type=f32) for rw in rows]
    x1 = [x_ref[rw, :] + _rms(mx, post_mix_ref[...]) for rw, mx in zip(rows, mix)]
    h = [_rms(xa, pre_ffn_ref[...]).astype(bf16) for xa in x1]
    gate_up = [(jnp.dot(ha, wgate_ref[...], preferred_element_type=f32),
                jnp.dot(ha, wup_ref[...], preferred_element_type=f32)) for ha in h]
    f = [(gate / (1.0 + jnp.exp(-gate)) * up).astype(bf16) for gate, up in gate_up]
    y = [jnp.dot(fa, wdown_ref[...], preferred_element_type=f32) for fa in f]
    for rw, xa, ya in zip(rows, x1, y):
        o_ref[rw, :] = xa + _rms(ya, post_ffn_ref[...])


def _outffn(x2, gla_o, diff_o, wo_g, wo_d, post_mix, pre_ffn, wgate, wup, wdown, post_ffn):
    m, d = x2.shape
    tm = TM_PROJ

    def row(n):
        return pl.BlockSpec((tm, n), lambda i: (i, 0))

    def whole(a):
        return pl.BlockSpec(a.shape, lambda i: (0, 0), pipeline_mode=pl.Buffered(1))

    return pl.pallas_call(
        _outffn_kernel,
        grid=(m // tm,),
        in_specs=[row(d), row(gla_o.shape[1]), row(diff_o.shape[1]),
                  whole(wo_g), whole(wo_d), whole(post_mix), whole(pre_ffn),
                  whole(wgate), whole(wup), whole(wdown), whole(post_ffn)],
        out_specs=row(d),
        out_shape=jax.ShapeDtypeStruct((m, d), f32),
        compiler_params=pltpu.CompilerParams(
            dimension_semantics=("parallel",), vmem_limit_bytes=VMEM_LIMIT_BYTES),
        name="outffn",
    )(x2, gla_o, diff_o, wo_g, wo_d, post_mix, pre_ffn, wgate, wup, wdown, post_ffn)


def kernel(x, w_in, w_gk_up, b_gk, lambda_q1, lambda_k1, lambda_q2, lambda_k2, rel_bias,
           g_gla_head, g_diff_head, w_out, pre_mix_g, post_mix_g, w_gate, w_up, w_down,
           pre_ffn_g, post_ffn_g):
    batch, seq, d = x.shape
    depth = w_in.shape[0]
    assert seq % T_ATTN == 0 and (batch * seq) % TM_PROJ == 0 and T_ATTN >= REL_MAX_DIST
    n_gla = 2 * GLA_HEADS * GLA_DK + 2 * GLA_HEADS * GLA_DV
    bias_tiles = _t5_bias_tiles(rel_bias, T_ATTN)
    x2 = x.reshape(batch * seq, d)
    for l in range(depth):
        lambda_init = 0.8 - 0.6 * math.exp(-0.3 * l)
        wg = w_in[l, :, :n_gla].astype(bf16)
        wl = jnp.pad(w_in[l, :, n_gla:n_gla + GLA_GATE_RANK],
                     ((0, 0), (0, LANES - GLA_GATE_RANK))).astype(bf16)
        n_dqk = 2 * DIFF_HEADS * 2 * DIFF_DK
        d0 = n_gla + GLA_GATE_RANK
        wdqk = w_in[l, :, d0:d0 + n_dqk].astype(bf16)
        wdvt = w_in[l, :, d0 + n_dqk:].T.astype(bf16)
        wup = jnp.pad(w_gk_up[l], ((0, LANES - GLA_GATE_RANK), (0, 0))).astype(bf16)
        gq, gk, gv, gog, loga, dq, dk, dvt = _inproj(
            x2, pre_mix_g[l][None, :], wg, wl, wdqk, wdvt, wup, b_gk[l][None, :])
        gla_o = _gla(gq, gk, gv, gog, loga, g_gla_head[l][None, :], batch, seq)
        diff_o = _diff_attn(dq, dk, dvt, bias_tiles,
                            lambda_q1[l][None, :], lambda_k1[l][None, :],
                            lambda_q2[l][None, :], lambda_k2[l][None, :],
                            g_diff_head[l][None, :], batch, seq, lambda_init)
        n_go = GLA_HEADS * GLA_DV
        x2 = _outffn(x2, gla_o, diff_o,
                     w_out[l, :n_go].astype(bf16), w_out[l, n_go:].astype(bf16),
                     post_mix_g[l][None, :], pre_ffn_g[l][None, :],
                     w_gate[l].astype(bf16), w_up[l].astype(bf16), w_down[l].astype(bf16),
                     post_ffn_g[l][None, :])
    return x2.reshape(batch, seq, d)
```

```python
import functools
import math

import jax
import jax.numpy as jnp
from jax import lax
from jax.experimental import pallas as pl
from jax.experimental.pallas import tpu as pltpu

GLA_HEADS = 4
GLA_DK = 64
GLA_DV = 128
GLA_GATE_RANK = 16
GLA_GATE_NORMALIZER = 16.0
GLA_CHUNK = 64
DIFF_HEADS = 4
DIFF_DK = 64
DIFF_DV = 128
REL_BUCKETS = 32
REL_MAX_DIST = 128
EPS = 1e-6

LANES = 128
LOG2E = 1.4426950408889634
NEG_BIG = -1e30
VMEM_LIMIT_BYTES = 56 * 1024 * 1024

TM_PROJ = 512
T_ATTN = 256
ATTN_HEADS_PER_STEP = 2
PROJ_ROW_SPLITS = (256,)
FFN_ROW_SPLITS = (256,)
GLA_BLOCK_CHUNKS = 8

f32 = jnp.float32
bf16 = jnp.bfloat16


def _rms(x, g):
    return x * lax.rsqrt(jnp.mean(x * x, axis=-1, keepdims=True) + EPS) * g


def _inproj_kernel(x_ref, g_ref, wg_ref, wl_ref, wd_ref, wup_ref, bgk_ref,
                   gq_ref, gk_ref, gv_ref, gog_ref, loga_ref,
                   dq_ref, dk_ref, dv_ref):
    nq = GLA_HEADS * GLA_DK
    nv = GLA_HEADS * GLA_DV
    nd = DIFF_HEADS * 2 * DIFF_DK
    bounds = (0,) + PROJ_ROW_SPLITS + (x_ref.shape[0],)
    rows = [slice(lo, hi) for lo, hi in zip(bounds[:-1], bounds[1:])]
    h = [_rms(x_ref[rw, :], g_ref[...]).astype(bf16) for rw in rows]
    low = [jnp.dot(ha, wl_ref[...], preferred_element_type=f32) for ha in h]
    for rw, lw in zip(rows, low):
        z = jnp.dot(lw.astype(bf16), wup_ref[...], preferred_element_type=f32) + bgk_ref[...]
        log_sig = jnp.minimum(z, 0.0) - jnp.log(1.0 + jnp.exp(-jnp.abs(z)))
        loga_ref[rw, :] = log_sig * (1.0 / GLA_GATE_NORMALIZER)
    for rw, ha in zip(rows, h):
        gla = jnp.dot(ha, wg_ref[...], preferred_element_type=f32)
        gq_ref[rw, :] = gla[:, :nq].astype(bf16)
        gk_ref[rw, :] = gla[:, nq:2 * nq].astype(bf16)
        gv_ref[rw, :] = gla[:, 2 * nq:2 * nq + nv].astype(bf16)
        gog_ref[rw, :] = gla[:, 2 * nq + nv:].astype(bf16)
    for rw, ha in zip(rows, h):
        d = jnp.dot(ha, wd_ref[...], preferred_element_type=f32)
        dq_ref[rw, :] = (d[:, :nd] * (DIFF_DK ** -0.5 * LOG2E)).astype(bf16)
        dk_ref[rw, :] = d[:, nd:2 * nd].astype(bf16)
        dv_ref[rw, :] = d[:, 2 * nd:].astype(bf16)


def _inproj(x2, g, wg, wl, wd, wup, bgk):
    m, d = x2.shape
    tm = TM_PROJ
    nq = GLA_HEADS * GLA_DK
    nv = GLA_HEADS * GLA_DV
    nd = DIFF_HEADS * 2 * DIFF_DK
    ndv = DIFF_HEADS * DIFF_DV

    def row(n):
        return pl.BlockSpec((tm, n), lambda i: (i, 0))

    def whole(a):
        return pl.BlockSpec(a.shape, lambda i: (0, 0))

    out_shapes = [
        jax.ShapeDtypeStruct((m, nq), bf16), jax.ShapeDtypeStruct((m, nq), bf16),
        jax.ShapeDtypeStruct((m, nv), bf16), jax.ShapeDtypeStruct((m, nv), bf16),
        jax.ShapeDtypeStruct((m, nq), f32),
        jax.ShapeDtypeStruct((m, nd), bf16), jax.ShapeDtypeStruct((m, nd), bf16),
        jax.ShapeDtypeStruct((m, ndv), bf16),
    ]
    return pl.pallas_call(
        _inproj_kernel,
        grid=(m // tm,),
        in_specs=[row(d), whole(g), whole(wg), whole(wl), whole(wd), whole(wup), whole(bgk)],
        out_specs=[row(nq), row(nq), row(nv), row(nv), row(nq), row(nd), row(nd), row(ndv)],
        out_shape=out_shapes,
        compiler_params=pltpu.CompilerParams(
            dimension_semantics=("parallel",), vmem_limit_bytes=VMEM_LIMIT_BYTES),
        name="inproj",
    )(x2, g, wg, wl, wd, wup, bgk)


def _gla_kernel(q_ref, k_ref, v_ref, og_ref, la_ref, g_ref, o_ref, *, seq):
    c = GLA_CHUNK
    row_i = lax.broadcasted_iota(jnp.int32, (c, c), 0)
    col_i = lax.broadcasted_iota(jnp.int32, (c, c), 1)
    tril = row_i >= col_i
    tril_ones = jnp.where(tril, 1.0, 0.0).astype(bf16)
    lane = lax.broadcasted_iota(jnp.int32, (c, LANES), 1)
    head0_lanes = lane < GLA_DK
    srow = lax.broadcasted_iota(jnp.int32, (2 * GLA_DV, LANES), 0)
    slane = lax.broadcasted_iota(jnp.int32, (2 * GLA_DV, LANES), 1)
    state_mask = (srow < GLA_DV) == (slane < GLA_DK)
    g = g_ref[...]

    nt = (((1,), (1,)), ((), ()))
    tn = (((0,), (0,)), ((), ()))
    nb = GLA_BLOCK_CHUNKS
    n_blocks = seq // (nb * c)

    def stage_a(blk):
        rows = [slice((blk * nb + n) * c, (blk * nb + n + 1) * c) for n in range(nb)]
        cum = []
        for n in range(nb):
            la = la_ref[rows[n], :]
            la_hi = la.astype(bf16)
            la_lo = (la - la_hi.astype(f32)).astype(bf16)
            cum.append(jnp.dot(tril_ones, la_hi, preferred_element_type=f32)
                       + jnp.dot(tril_ones, la_lo, preferred_element_type=f32))
        return rows, cum

    def stage_b(rows, cum):
        q_heads, k_intra, k_state, q_inter, decay = [], [], [], [], []
        for n in range(nb):
            b = cum[n]
            b_last = b[c - 1:c, :]
            b_mid = b[c // 2 - 1:c // 2, :]
            q = q_ref[rows[n], :].astype(f32) * (GLA_DK ** -0.5)
            k = k_ref[rows[n], :].astype(f32)
            qi = (q * jnp.exp(b - b_mid)).astype(bf16)
            zero = jnp.zeros_like(qi)
            q_heads.append(jnp.concatenate([jnp.where(head0_lanes, qi, zero),
                                            jnp.where(head0_lanes, zero, qi)], axis=0))
            k_intra.append((k * jnp.exp(b_mid - b)).astype(bf16))
            k_state.append((k * jnp.exp(b_last - b)).astype(bf16))
            q_inter.append((q * jnp.exp(b)).astype(bf16))
            decay.append(jnp.exp(b_last))
        scores = [lax.dot_general(q_heads[n], k_intra[n], nt, preferred_element_type=f32)
                  for n in range(nb)]
        u_t = [lax.dot_general(v_ref[rows[n], :], k_state[n], tn, preferred_element_type=f32)
               for n in range(nb)]
        return scores, u_t, q_inter, decay

    def stage_c(rows, scores, u_t, q_inter, decay, state_t):
        states = []
        for n in range(nb):
            states.append(state_t.astype(bf16))
            state_t = state_t * decay[n] + jnp.where(state_mask, u_t[n], 0.0)
        o_inter = [lax.dot_general(q_inter[n], states[n], nt, preferred_element_type=f32)
                   for n in range(nb)]
        for n in range(nb):
            outs = []
            for hh in range(2):
                cols = slice(hh * GLA_DV, (hh + 1) * GLA_DV)
                s_h = jnp.where(tril, scores[n][hh * c:(hh + 1) * c, :], 0.0).astype(bf16)
                o_h = o_inter[n][:, cols] + jnp.dot(s_h, v_ref[rows[n], cols],
                                                    preferred_element_type=f32)
                og_h = og_ref[rows[n], cols].astype(f32)
                silu = og_h / (1.0 + jnp.exp(-og_h))
                outs.append((_rms(o_h, g) * silu).astype(bf16))
            o_ref[rows[n], :] = jnp.concatenate(outs, axis=1)
        return state_t

    state_t = jnp.zeros((2 * GLA_DV, LANES), f32)
    rows, cum = stage_a(0)
    for blk in range(n_blocks):
        mid = stage_b(rows, cum)
        cur_rows = rows
        if blk + 1 < n_blocks:
            rows, cum = stage_a(blk + 1)
        state_t = stage_c(cur_rows, *mid, state_t)


def _gla(gq, gk, gv, gog, loga, g_head, batch, seq):
    m = gq.shape[0]
    pairs = GLA_HEADS // 2
    kern = functools.partial(_gla_kernel, seq=seq)
    return pl.pallas_call(
        kern,
        grid=(batch, pairs),
        in_specs=[
            pl.BlockSpec((seq, LANES), lambda b, p: (b, p)),
            pl.BlockSpec((seq, LANES), lambda b, p: (b, p)),
            pl.BlockSpec((seq, 2 * GLA_DV), lambda b, p: (b, p)),
            pl.BlockSpec((seq, 2 * GLA_DV), lambda b, p: (b, p)),
            pl.BlockSpec((seq, LANES), lambda b, p: (b, p)),
            pl.BlockSpec((1, GLA_DV), lambda b, p: (0, 0)),
        ],
        out_specs=pl.BlockSpec((seq, 2 * GLA_DV), lambda b, p: (b, p)),
        out_shape=jax.ShapeDtypeStruct((m, GLA_HEADS * GLA_DV), bf16),
        compiler_params=pltpu.CompilerParams(
            dimension_semantics=("parallel", "parallel"), vmem_limit_bytes=VMEM_LIMIT_BYTES),
        name="gla",
    )(gq, gk, gv, gog, loga, g_head)


def _diff_kernel(q_ref, k_ref, v_ref, bias_ref, lq1_ref, lk1_ref, lq2_ref, lk2_ref, g_ref,
                 o_ref, vext_ref, *, lambda_init, seq):
    t = T_ATTN
    nq = seq // t
    heads = ATTN_HEADS_PER_STEP
    for hd in range(heads):
        vext_ref[hd, :, :DIFF_DV] = v_ref[:, hd * DIFF_DV:(hd + 1) * DIFF_DV]
        vext_ref[hd, :, DIFF_DV:] = jnp.ones((seq, DIFF_DV), bf16)
    lane = lax.broadcasted_iota(jnp.int32, (t, LANES), 1)
    map1_lanes = lane < DIFF_DK
    lam = (jnp.exp(jnp.sum(lq1_ref[...] * lk1_ref[...], axis=-1, keepdims=True))
           - jnp.exp(jnp.sum(lq2_ref[...] * lk2_ref[...], axis=-1, keepdims=True))
           + lambda_init)
    g = g_ref[...]
    nt = (((1,), (1,)), ((), ()))

    def key_ranges(i):
        pieces = []
        if i >= 2:
            pieces.append((0, (i - 1) * t, None))
        if i >= 1:
            pieces.append(((i - 1) * t, i * t, 1))
        pieces.append((i * t, (i + 1) * t, 0))
        return pieces

    def qk(hd, i):
        cols = slice(hd * LANES, (hd + 1) * LANES)
        q = q_ref[i * t:(i + 1) * t, cols]
        zero = jnp.zeros_like(q)
        qs = jnp.concatenate([jnp.where(map1_lanes, q, zero),
                              jnp.where(map1_lanes, zero, q)], axis=0)
        logits = []
        for lo, hi, bias in key_ranges(i):
            s = lax.dot_general(qs, k_ref[lo:hi, cols], nt, preferred_element_type=f32)
            if bias is not None:
                tile = bias_ref[hd, bias]
                s = s + jnp.concatenate([tile, tile], axis=0)
            logits.append(s)
        return logits

    units = [(hd, i) for hd in range(heads) for i in range(nq)]
    logits = qk(*units[0])
    for n, (hd, i) in enumerate(units):
        nxt = qk(*units[n + 1]) if n + 1 < len(units) else None
        m = functools.reduce(jnp.maximum, [jnp.max(s, axis=-1, keepdims=True) for s in logits])
        pv = None
        for (lo, hi, _), s in zip(key_ranges(i), logits):
            p = jnp.exp2(s - m).astype(bf16)
            part = jnp.dot(p, vext_ref[hd, lo:hi, :], preferred_element_type=f32)
            pv = part if pv is None else pv + part
        o = pv[:, :DIFF_DV] / pv[:, DIFF_DV:]
        o = o[:t, :] - lam * o[t:, :]
        o_ref[i * t:(i + 1) * t, hd * DIFF_DV:(hd + 1) * DIFF_DV] = (
            _rms(o, g) * (1.0 - lambda_init)).astype(bf16)
        logits = nxt


def _diff_attn(dq, dk, dv, bias_tiles, lq1, lk1, lq2, lk2, g_head, batch, seq, lambda_init):
    m = dq.shape[0]
    t = T_ATTN
    heads = ATTN_HEADS_PER_STEP
    kern = functools.partial(_diff_kernel, lambda_init=lambda_init, seq=seq)
    vec = pl.BlockSpec((1, DIFF_DK), lambda b, h: (0, 0))
    head_rows = pl.BlockSpec((seq, heads * LANES), lambda b, h: (b, h))
    return pl.pallas_call(
        kern,
        grid=(batch, DIFF_HEADS // heads),
        in_specs=[
            head_rows, head_rows, head_rows,
            pl.BlockSpec((heads, 2, t, t), lambda b, h: (h, 0, 0, 0)),
            vec, vec, vec, vec,
            pl.BlockSpec((1, DIFF_DV), lambda b, h: (0, 0)),
        ],
        out_specs=head_rows,
        out_shape=jax.ShapeDtypeStruct((m, DIFF_HEADS * DIFF_DV), bf16),
        scratch_shapes=[pltpu.VMEM((heads, seq, 2 * DIFF_DV), bf16)],
        compiler_params=pltpu.CompilerParams(
            dimension_semantics=("parallel", "parallel"),
            vmem_limit_bytes=VMEM_LIMIT_BYTES),
        name="diff_attn",
    )(dq, dk, dv, bias_tiles, lq1, lk1, lq2, lk2, g_head)


def _toeplitz(g, t):
    w = jnp.concatenate([g[1:t + 1][::-1], g[:1], g[t + 1:][::-1]])
    skew = jnp.broadcast_to(w, (t, 2 * t)).reshape(-1)[:t * (2 * t - 1)]
    return skew.reshape(t, 2 * t - 1)[:, :t]


def _t5_bias_tiles(rel_bias, t):
    rel = jnp.arange(2 * t)
    max_exact = REL_BUCKETS // 2
    nf = jnp.maximum(rel, 1).astype(f32)
    large = max_exact + (jnp.log(nf / max_exact) / math.log(REL_MAX_DIST / max_exact)
                         * (REL_BUCKETS - max_exact)).astype(jnp.int32)
    large = jnp.minimum(large, REL_BUCKETS - 1)
    bucket = jnp.where(rel < max_exact, rel, large)
    table = rel_bias.astype(f32)
    bvec = (table[bucket] - table[REL_BUCKETS - 1][None, :]) * LOG2E
    tiles = []
    for h in range(DIFF_HEADS):
        b = bvec[:, h]
        diag = jnp.concatenate([jnp.full((t,), NEG_BIG, f32), b[:t]])
        tiles.append(jnp.stack([_toeplitz(diag, t), _toeplitz(b, t)]))
    return jnp.stack(tiles)


def _outffn_kernel(x_ref, gla_ref, diff_ref, wo_g_ref, wo_d_ref, post_mix_ref, pre_ffn_ref,
                   wgate_ref, wup_ref, wdown_ref, post_ffn_ref, o_ref):
    bounds = (0,) + FFN_ROW_SPLITS + (x_ref.shape[0],)
    rows = [slice(lo, hi) for lo, hi in zip(bounds[:-1], bounds[1:])]
    mix = [jnp.dot(gla_ref[rw, :], wo_g_ref[...], preferred_element_type=f32)
           + jnp.dot(diff_ref[rw, :], wo_d_ref[...], preferred_element_type=f32) for rw in rows]
    x1 = [x_ref[rw, :] + _rms(mx, post_mix_ref[...]) for rw, mx in zip(rows, mix)]
    h = [_rms(xa, pre_ffn_ref[...]).astype(bf16) for xa in x1]
    gate_up = [(jnp.dot(ha, wgate_ref[...], preferred_element_type=f32),
                jnp.dot(ha, wup_ref[...], preferred_element_type=f32)) for ha in h]
    f = [(gate / (1.0 + jnp.exp(-gate)) * up).astype(bf16) for gate, up in gate_up]
    y = [jnp.dot(fa, wdown_ref[...], preferred_element_type=f32) for fa in f]
    for rw, xa, ya in zip(rows, x1, y):
        o_ref[rw, :] = xa + _rms(ya, post_ffn_ref[...])


def _outffn(x2, gla_o, diff_o, wo_g, wo_d, post_mix, pre_ffn, wgate, wup, wdown, post_ffn):
    m, d = x2.shape
    tm = TM_PROJ

    def row(n):
        return pl.BlockSpec((tm, n), lambda i: (i, 0))

    def whole(a):
        return pl.BlockSpec(a.shape, lambda i: (0, 0), pipeline_mode=pl.Buffered(1))

    return pl.pallas_call(
        _outffn_kernel,
        grid=(m // tm,),
        in_specs=[row(d), row(gla_o.shape[1]), row(diff_o.shape[1]),
                  whole(wo_g), whole(wo_d), whole(post_mix), whole(pre_ffn),
                  whole(wgate), whole(wup), whole(wdown), whole(post_ffn)],
        out_specs=row(d),
        out_shape=jax.ShapeDtypeStruct((m, d), f32),
        compiler_params=pltpu.CompilerParams(
            dimension_semantics=("parallel",), vmem_limit_bytes=VMEM_LIMIT_BYTES),
        name="outffn",
    )(x2, gla_o, diff_o, wo_g, wo_d, post_mix, pre_ffn, wgate, wup, wdown, post_ffn)


def kernel(x, w_in, w_gk_up, b_gk, lambda_q1, lambda_k1, lambda_q2, lambda_k2, rel_bias,
           g_gla_head, g_diff_head, w_out, pre_mix_g, post_mix_g, w_gate, w_up, w_down,
           pre_ffn_g, post_ffn_g):
    batch, seq, d = x.shape
    depth = w_in.shape[0]
    assert seq % T_ATTN == 0 and (batch * seq) % TM_PROJ == 0 and T_ATTN >= REL_MAX_DIST
    assert seq % (GLA_BLOCK_CHUNKS * GLA_CHUNK) == 0
    n_gla = 2 * GLA_HEADS * GLA_DK + 2 * GLA_HEADS * GLA_DV
    bias_tiles = _t5_bias_tiles(rel_bias, T_ATTN)
    x2 = x.reshape(batch * seq, d)
    for l in range(depth):
        lambda_init = 0.8 - 0.6 * math.exp(-0.3 * l)
        wg = w_in[l, :, :n_gla].astype(bf16)
        wl = jnp.pad(w_in[l, :, n_gla:n_gla + GLA_GATE_RANK],
                     ((0, 0), (0, LANES - GLA_GATE_RANK))).astype(bf16)
        wd = w_in[l, :, n_gla + GLA_GATE_RANK:].astype(bf16)
        wup = jnp.pad(w_gk_up[l], ((0, LANES - GLA_GATE_RANK), (0, 0))).astype(bf16)
        gq, gk, gv, gog, loga, dq, dk, dv = _inproj(
            x2, pre_mix_g[l][None, :], wg, wl, wd, wup, b_gk[l][None, :])
        gla_o = _gla(gq, gk, gv, gog, loga, g_gla_head[l][None, :], batch, seq)
        diff_o = _diff_attn(dq, dk, dv, bias_tiles,
                            lambda_q1[l][None, :], lambda_k1[l][None, :],
                            lambda_q2[l][None, :], lambda_k2[l][None, :],
                            g_diff_head[l][None, :], batch, seq, lambda_init)
        n_go = GLA_HEADS * GLA_DV
        x2 = _outffn(x2, gla_o, diff_o,
                     w_out[l, :n_go].astype(bf16), w_out[l, n_go:].astype(bf16),
                     post_mix_g[l][None, :], pre_ffn_g[l][None, :],
                     w_gate[l].astype(bf16), w_up[l].astype(bf16), w_down[l].astype(bf16),
                     post_ffn_g[l][None, :])
    return x2.reshape(batch, seq, d)
```

```python
import functools
import math

import jax
import jax.numpy as jnp
from jax import lax
from jax.experimental import pallas as pl
from jax.experimental.pallas import tpu as pltpu

GLA_HEADS = 4
GLA_DK = 64
GLA_DV = 128
GLA_GATE_RANK = 16
GLA_GATE_NORMALIZER = 16.0
GLA_CHUNK = 64
DIFF_HEADS = 4
DIFF_DK = 64
DIFF_DV = 128
REL_BUCKETS = 32
REL_MAX_DIST = 128
EPS = 1e-6

LANES = 128
LOG2E = 1.4426950408889634
NEG_BIG = -1e30
VMEM_LIMIT_BYTES = 56 * 1024 * 1024

TM_INPROJ = 1024
TM_FFN = 512
T_ATTN = 256
ATTN_HEADS_PER_STEP = 2
PROJ_ROW_SPLITS = (256, 512, 768)
FFN_ROW_SPLITS = (256,)
GLA_BLOCK_CHUNKS = 8

f32 = jnp.float32
bf16 = jnp.bfloat16


def _rms(x, g):
    return x * lax.rsqrt(jnp.mean(x * x, axis=-1, keepdims=True) + EPS) * g


def _inproj_kernel(x_ref, g_ref, wg_ref, wl_ref, wd_ref, wup_ref, bgk_ref,
                   gq_ref, gk_ref, gv_ref, gog_ref, loga_ref,
                   dq_ref, dk_ref, dv_ref):
    nq = GLA_HEADS * GLA_DK
    nv = GLA_HEADS * GLA_DV
    nd = DIFF_HEADS * 2 * DIFF_DK
    bounds = (0,) + PROJ_ROW_SPLITS + (x_ref.shape[0],)
    rows = [slice(lo, hi) for lo, hi in zip(bounds[:-1], bounds[1:])]
    h = [_rms(x_ref[rw, :], g_ref[...]).astype(bf16) for rw in rows]
    low = [jnp.dot(ha, wl_ref[...], preferred_element_type=f32) for ha in h]
    for rw, lw in zip(rows, low):
        z = jnp.dot(lw.astype(bf16), wup_ref[...], preferred_element_type=f32) + bgk_ref[...]
        log_sig = jnp.minimum(z, 0.0) - jnp.log(1.0 + jnp.exp(-jnp.abs(z)))
        loga_ref[rw, :] = log_sig * (1.0 / GLA_GATE_NORMALIZER)
    for rw, ha in zip(rows, h):
        gla = jnp.dot(ha, wg_ref[...], preferred_element_type=f32)
        gq_ref[rw, :] = gla[:, :nq].astype(bf16)
        gk_ref[rw, :] = gla[:, nq:2 * nq].astype(bf16)
        gv_ref[rw, :] = gla[:, 2 * nq:2 * nq + nv].astype(bf16)
        gog_ref[rw, :] = gla[:, 2 * nq + nv:].astype(bf16)
    for rw, ha in zip(rows, h):
        d = jnp.dot(ha, wd_ref[...], preferred_element_type=f32)
        dq_ref[rw, :] = (d[:, :nd] * (DIFF_DK ** -0.5 * LOG2E)).astype(bf16)
        dk_ref[rw, :] = d[:, nd:2 * nd].astype(bf16)
        dv_ref[rw, :] = d[:, 2 * nd:].astype(bf16)


def _inproj(x2, g, wg, wl, wd, wup, bgk):
    m, d = x2.shape
    tm = TM_INPROJ
    nq = GLA_HEADS * GLA_DK
    nv = GLA_HEADS * GLA_DV
    nd = DIFF_HEADS * 2 * DIFF_DK
    ndv = DIFF_HEADS * DIFF_DV

    def row(n):
        return pl.BlockSpec((tm, n), lambda i: (i, 0))

    def whole(a):
        return pl.BlockSpec(a.shape, lambda i: (0, 0), pipeline_mode=pl.Buffered(1))

    out_shapes = [
        jax.ShapeDtypeStruct((m, nq), bf16), jax.ShapeDtypeStruct((m, nq), bf16),
        jax.ShapeDtypeStruct((m, nv), bf16), jax.ShapeDtypeStruct((m, nv), bf16),
        jax.ShapeDtypeStruct((m, nq), f32),
        jax.ShapeDtypeStruct((m, nd), bf16), jax.ShapeDtypeStruct((m, nd), bf16),
        jax.ShapeDtypeStruct((m, ndv), bf16),
    ]
    return pl.pallas_call(
        _inproj_kernel,
        grid=(m // tm,),
        in_specs=[row(d), whole(g), whole(wg), whole(wl), whole(wd), whole(wup), whole(bgk)],
        out_specs=[row(nq), row(nq), row(nv), row(nv), row(nq), row(nd), row(nd), row(ndv)],
        out_shape=out_shapes,
        compiler_params=pltpu.CompilerParams(
            dimension_semantics=("parallel",), vmem_limit_bytes=VMEM_LIMIT_BYTES),
        name="inproj",
    )(x2, g, wg, wl, wd, wup, bgk)


def _gla_kernel(q_ref, k_ref, v_ref, og_ref, la_ref, g_ref, o_ref, *, seq):
    c = GLA_CHUNK
    row_i = lax.broadcasted_iota(jnp.int32, (c, c), 0)
    col_i = lax.broadcasted_iota(jnp.int32, (c, c), 1)
    tril = row_i >= col_i
    tril_ones = jnp.where(tril, 1.0, 0.0).astype(bf16)
    lane = lax.broadcasted_iota(jnp.int32, (c, LANES), 1)
    head0_lanes = lane < GLA_DK
    srow = lax.broadcasted_iota(jnp.int32, (2 * GLA_DV, LANES), 0)
    slane = lax.broadcasted_iota(jnp.int32, (2 * GLA_DV, LANES), 1)
    state_mask = (srow < GLA_DV) == (slane < GLA_DK)
    g = g_ref[...]

    nt = (((1,), (1,)), ((), ()))
    tn = (((0,), (0,)), ((), ()))
    nb = GLA_BLOCK_CHUNKS
    n_blocks = seq // (nb * c)

    def stage_a(blk):
        rows = [slice((blk * nb + n) * c, (blk * nb + n + 1) * c) for n in range(nb)]
        cum = []
        for n in range(nb):
            la = la_ref[rows[n], :]
            la_hi = la.astype(bf16)
            la_lo = (la - la_hi.astype(f32)).astype(bf16)
            cum.append(jnp.dot(tril_ones, la_hi, preferred_element_type=f32)
                       + jnp.dot(tril_ones, la_lo, preferred_element_type=f32))
        return rows, cum

    def stage_b(rows, cum):
        q_heads, k_intra, k_state, q_inter, decay = [], [], [], [], []
        for n in range(nb):
            b = cum[n]
            b_last = b[c - 1:c, :]
            b_mid = b[c // 2 - 1:c // 2, :]
            q = q_ref[rows[n], :].astype(f32) * (GLA_DK ** -0.5)
            k = k_ref[rows[n], :].astype(f32)
            qi = (q * jnp.exp(b - b_mid)).astype(bf16)
            zero = jnp.zeros_like(qi)
            q_heads.append(jnp.concatenate([jnp.where(head0_lanes, qi, zero),
                                            jnp.where(head0_lanes, zero, qi)], axis=0))
            k_intra.append((k * jnp.exp(b_mid - b)).astype(bf16))
            k_state.append((k * jnp.exp(b_last - b)).astype(bf16))
            q_inter.append((q * jnp.exp(b)).astype(bf16))
            decay.append(jnp.exp(b_last))
        scores = [lax.dot_general(q_heads[n], k_intra[n], nt, preferred_element_type=f32)
                  for n in range(nb)]
        u_t = [lax.dot_general(v_ref[rows[n], :], k_state[n], tn, preferred_element_type=f32)
               for n in range(nb)]
        return scores, u_t, q_inter, decay

    def stage_c(rows, scores, u_t, q_inter, decay, state_t):
        states = []
        for n in range(nb):
            states.append(state_t.astype(bf16))
            state_t = state_t * decay[n] + jnp.where(state_mask, u_t[n], 0.0)
        o_inter = [lax.dot_general(q_inter[n], states[n], nt, preferred_element_type=f32)
                   for n in range(nb)]
        for n in range(nb):
            outs = []
            for hh in range(2):
                cols = slice(hh * GLA_DV, (hh + 1) * GLA_DV)
                s_h = jnp.where(tril, scores[n][hh * c:(hh + 1) * c, :], 0.0).astype(bf16)
                o_h = o_inter[n][:, cols] + jnp.dot(s_h, v_ref[rows[n], cols],
                                                    preferred_element_type=f32)
                og_h = og_ref[rows[n], cols].astype(f32)
                silu = og_h / (1.0 + jnp.exp(-og_h))
                outs.append((_rms(o_h, g) * silu).astype(bf16))
            o_ref[rows[n], :] = jnp.concatenate(outs, axis=1)
        return state_t

    state_t = jnp.zeros((2 * GLA_DV, LANES), f32)
    rows, cum = stage_a(0)
    for blk in range(n_blocks):
        mid = stage_b(rows, cum)
        cur_rows = rows
        if blk + 1 < n_blocks:
            rows, cum = stage_a(blk + 1)
        state_t = stage_c(cur_rows, *mid, state_t)


def _gla(gq, gk, gv, gog, loga, g_head, batch, seq):
    m = gq.shape[0]
    pairs = GLA_HEADS // 2
    kern = functools.partial(_gla_kernel, seq=seq)
    return pl.pallas_call(
        kern,
        grid=(batch, pairs),
        in_specs=[
            pl.BlockSpec((seq, LANES), lambda b, p: (b, p)),
            pl.BlockSpec((seq, LANES), lambda b, p: (b, p)),
            pl.BlockSpec((seq, 2 * GLA_DV), lambda b, p: (b, p)),
            pl.BlockSpec((seq, 2 * GLA_DV), lambda b, p: (b, p)),
            pl.BlockSpec((seq, LANES), lambda b, p: (b, p)),
            pl.BlockSpec((1, GLA_DV), lambda b, p: (0, 0)),
        ],
        out_specs=pl.BlockSpec((seq, 2 * GLA_DV), lambda b, p: (b, p)),
        out_shape=jax.ShapeDtypeStruct((m, GLA_HEADS * GLA_DV), bf16),
        compiler_params=pltpu.CompilerParams(
            dimension_semantics=("parallel", "parallel"), vmem_limit_bytes=VMEM_LIMIT_BYTES),
        name="gla",
    )(gq, gk, gv, gog, loga, g_head)


def _diff_kernel(q_ref, k_ref, v_ref, bias_ref, lq1_ref, lk1_ref, lq2_ref, lk2_ref, g_ref,
                 o_ref, vext_ref, *, lambda_init, seq):
    t = T_ATTN
    nq = seq // t
    heads = ATTN_HEADS_PER_STEP
    for hd in range(heads):
        vext_ref[hd, :, :DIFF_DV] = v_ref[:, hd * DIFF_DV:(hd + 1) * DIFF_DV]
        vext_ref[hd, :, DIFF_DV:] = jnp.ones((seq, DIFF_DV), bf16)
    lane = lax.broadcasted_iota(jnp.int32, (t, LANES), 1)
    map1_lanes = lane < DIFF_DK
    lam = (jnp.exp(jnp.sum(lq1_ref[...] * lk1_ref[...], axis=-1, keepdims=True))
           - jnp.exp(jnp.sum(lq2_ref[...] * lk2_ref[...], axis=-1, keepdims=True))
           + lambda_init)
    g = g_ref[...]
    nt = (((1,), (1,)), ((), ()))

    def key_ranges(i):
        pieces = []
        if i >= 2:
            pieces.append((0, (i - 1) * t, None))
        if i >= 1:
            pieces.append(((i - 1) * t, i * t, 1))
        pieces.append((i * t, (i + 1) * t, 0))
        return pieces

    def qk(hd, i):
        cols = slice(hd * LANES, (hd + 1) * LANES)
        q = q_ref[i * t:(i + 1) * t, cols]
        zero = jnp.zeros_like(q)
        qs = jnp.concatenate([jnp.where(map1_lanes, q, zero),
                              jnp.where(map1_lanes, zero, q)], axis=0)
        logits = []
        for lo, hi, bias in key_ranges(i):
            s = lax.dot_general(qs, k_ref[lo:hi, cols], nt, preferred_element_type=f32)
            if bias is not None:
                tile = bias_ref[hd, bias]
                s = s + jnp.concatenate([tile, tile], axis=0)
            logits.append(s)
        return logits

    units = [(hd, i) for hd in range(heads) for i in range(nq)]
    logits = qk(*units[0])
    for n, (hd, i) in enumerate(units):
        nxt = qk(*units[n + 1]) if n + 1 < len(units) else None
        m = functools.reduce(jnp.maximum, [jnp.max(s, axis=-1, keepdims=True) for s in logits])
        pv = None
        for (lo, hi, _), s in zip(key_ranges(i), logits):
            p = jnp.exp2(s - m).astype(bf16)
            part = jnp.dot(p, vext_ref[hd, lo:hi, :], preferred_element_type=f32)
            pv = part if pv is None else pv + part
        o = pv[:, :DIFF_DV] / pv[:, DIFF_DV:]
        o = o[:t, :] - lam * o[t:, :]
        o_ref[i * t:(i + 1) * t, hd * DIFF_DV:(hd + 1) * DIFF_DV] = (
            _rms(o, g) * (1.0 - lambda_init)).astype(bf16)
        logits = nxt


def _diff_attn(dq, dk, dv, bias_tiles, lq1, lk1, lq2, lk2, g_head, batch, seq, lambda_init):
    m = dq.shape[0]
    t = T_ATTN
    heads = ATTN_HEADS_PER_STEP
    kern = functools.partial(_diff_kernel, lambda_init=lambda_init, seq=seq)
    vec = pl.BlockSpec((1, DIFF_DK), lambda b, h: (0, 0))
    head_rows = pl.BlockSpec((seq, heads * LANES), lambda b, h: (b, h))
    return pl.pallas_call(
        kern,
        grid=(batch, DIFF_HEADS // heads),
        in_specs=[
            head_rows, head_rows, head_rows,
            pl.BlockSpec((heads, 2, t, t), lambda b, h: (h, 0, 0, 0)),
            vec, vec, vec, vec,
            pl.BlockSpec((1, DIFF_DV), lambda b, h: (0, 0)),
        ],
        out_specs=head_rows,
        out_shape=jax.ShapeDtypeStruct((m, DIFF_HEADS * DIFF_DV), bf16),
        scratch_shapes=[pltpu.VMEM((heads, seq, 2 * DIFF_DV), bf16)],
        compiler_params=pltpu.CompilerParams(
            dimension_semantics=("parallel", "parallel"),
            vmem_limit_bytes=VMEM_LIMIT_BYTES),
        name="diff_attn",
    )(dq, dk, dv, bias_tiles, lq1, lk1, lq2, lk2, g_head)


def _toeplitz(g, t):
    lead = g.shape[:-1]
    w = jnp.concatenate([g[..., 1:t + 1][..., ::-1], g[..., :1], g[..., t + 1:][..., ::-1]],
                        axis=-1)
    skew = jnp.broadcast_to(w[..., None, :], lead + (t, 2 * t)).reshape(lead + (-1,))
    return skew[..., :t * (2 * t - 1)].reshape(lead + (t, 2 * t - 1))[..., :t]


def _t5_bias_tiles(rel_bias, t):
    rel = jnp.arange(2 * t)
    max_exact = REL_BUCKETS // 2
    nf = jnp.maximum(rel, 1).astype(f32)
    large = max_exact + (jnp.log(nf / max_exact) / math.log(REL_MAX_DIST / max_exact)
                         * (REL_BUCKETS - max_exact)).astype(jnp.int32)
    large = jnp.minimum(large, REL_BUCKETS - 1)
    bucket = jnp.where(rel < max_exact, rel, large)
    table = rel_bias.astype(f32)
    by_dist = ((table[bucket] - table[REL_BUCKETS - 1][None, :]) * LOG2E).T
    masked = jnp.full((DIFF_HEADS, t), NEG_BIG, f32)
    diag = jnp.concatenate([masked, by_dist[:, :t]], axis=-1)
    return _toeplitz(jnp.stack([diag, by_dist], axis=1), t)


def _outffn_kernel(x_ref, gla_ref, diff_ref, wo_g_ref, wo_d_ref, post_mix_ref, pre_ffn_ref,
                   wgate_ref, wup_ref, wdown_ref, post_ffn_ref, o_ref):
    bounds = (0,) + FFN_ROW_SPLITS + (x_ref.shape[0],)
    rows = [slice(lo, hi) for lo, hi in zip(bounds[:-1], bounds[1:])]
    mix = [jnp.dot(gla_ref[rw, :], wo_g_ref[...], preferred_element_type=f32)
           + jnp.dot(diff_ref[rw, :], wo_d_ref[...], preferred_element_type=f32) for rw in rows]
    x1 = [x_ref[rw, :] + _rms(mx, post_mix_ref[...]) for rw, mx in zip(rows, mix)]
    h = [_rms(xa, pre_ffn_ref[...]).astype(bf16) for xa in x1]
    gate_up = [(jnp.dot(ha, wgate_ref[...], preferred_element_type=f32),
                jnp.dot(ha, wup_ref[...], preferred_element_type=f32)) for ha in h]
    f = [(gate / (1.0 + jnp.exp(-gate)) * up).astype(bf16) for gate, up in gate_up]
    y = [jnp.dot(fa, wdown_ref[...], preferred_element_type=f32) for fa in f]
    for rw, xa, ya in zip(rows, x1, y):
        o_ref[rw, :] = xa + _rms(ya, post_ffn_ref[...])


def _outffn(x2, gla_o, diff_o, wo_g, wo_d, post_mix, pre_ffn, wgate, wup, wdown, post_ffn):
    m, d = x2.shape
    tm = TM_FFN

    def row(n):
        return pl.BlockSpec((tm, n), lambda i: (i, 0))

    def whole(a):
        return pl.BlockSpec(a.shape, lambda i: (0, 0), pipeline_mode=pl.Buffered(1))

    return pl.pallas_call(
        _outffn_kernel,
        grid=(m // tm,),
        in_specs=[row(d), row(gla_o.shape[1]), row(diff_o.shape[1]),
                  whole(wo_g), whole(wo_d), whole(post_mix), whole(pre_ffn),
                  whole(wgate), whole(wup), whole(wdown), whole(post_ffn)],
        out_specs=row(d),
        out_shape=jax.ShapeDtypeStruct((m, d), f32),
        compiler_params=pltpu.CompilerParams(
            dimension_semantics=("parallel",), vmem_limit_bytes=VMEM_LIMIT_BYTES),
        name="outffn",
    )(x2, gla_o, diff_o, wo_g, wo_d, post_mix, pre_ffn, wgate, wup, wdown, post_ffn)


def kernel(x, w_in, w_gk_up, b_gk, lambda_q1, lambda_k1, lambda_q2, lambda_k2, rel_bias,
           g_gla_head, g_diff_head, w_out, pre_mix_g, post_mix_g, w_gate, w_up, w_down,
           pre_ffn_g, post_ffn_g):
    batch, seq, d = x.shape
    depth = w_in.shape[0]
    assert seq % T_ATTN == 0 and T_ATTN >= REL_MAX_DIST
    assert (batch * seq) % TM_INPROJ == 0 and (batch * seq) % TM_FFN == 0
    assert seq % (GLA_BLOCK_CHUNKS * GLA_CHUNK) == 0
    n_gla = 2 * GLA_HEADS * GLA_DK + 2 * GLA_HEADS * GLA_DV
    bias_tiles = _t5_bias_tiles(rel_bias, T_ATTN)
    x2 = x.reshape(batch * seq, d)
    for l in range(depth):
        lambda_init = 0.8 - 0.6 * math.exp(-0.3 * l)
        wg = w_in[l, :, :n_gla].astype(bf16)
        wl = jnp.pad(w_in[l, :, n_gla:n_gla + GLA_GATE_RANK],
                     ((0, 0), (0, LANES - GLA_GATE_RANK))).astype(bf16)
        wd = w_in[l, :, n_gla + GLA_GATE_RANK:].astype(bf16)
        wup = jnp.pad(w_gk_up[l], ((0, LANES - GLA_GATE_RANK), (0, 0))).astype(bf16)
        gq, gk, gv, gog, loga, dq, dk, dv = _inproj(
            x2, pre_mix_g[l][None, :], wg, wl, wd, wup, b_gk[l][None, :])
        gla_o = _gla(gq, gk, gv, gog, loga, g_gla_head[l][None, :], batch, seq)
        diff_o = _diff_attn(dq, dk, dv, bias_tiles,
                            lambda_q1[l][None, :], lambda_k1[l][None, :],
                            lambda_q2[l][None, :], lambda_k2[l][None, :],
                            g_diff_head[l][None, :], batch, seq, lambda_init)
        n_go = GLA_HEADS * GLA_DV
        x2 = _outffn(x2, gla_o, diff_o,
                     w_out[l, :n_go].astype(bf16), w_out[l, n_go:].astype(bf16),
                     post_mix_g[l][None, :], pre_ffn_g[l][None, :],
                     w_gate[l].astype(bf16), w_up[l].astype(bf16), w_down[l].astype(bf16),
                     post_ffn_g[l][None, :])
    return x2.reshape(batch, seq, d)
```

```python
import functools
import math

import jax
import jax.numpy as jnp
from jax import lax
from jax.experimental import pallas as pl
from jax.experimental.pallas import tpu as pltpu

GLA_HEADS = 4
GLA_DK = 64
GLA_DV = 128
GLA_GATE_RANK = 16
GLA_GATE_NORMALIZER = 16.0
GLA_CHUNK = 64
DIFF_HEADS = 4
DIFF_DK = 64
DIFF_DV = 128
REL_BUCKETS = 32
REL_MAX_DIST = 128
EPS = 1e-6

LANES = 128
LOG2E = 1.4426950408889634
NEG_BIG = -1e30
VMEM_LIMIT_BYTES = 56 * 1024 * 1024

TM_INPROJ = 1024
TM_FFN = 1024
T_ATTN = 256
ATTN_HEADS_PER_STEP = 2
PROJ_ROW_SPLITS = (256, 512, 768)
FFN_ROW_SPLITS = (256, 512, 768)
GLA_BLOCK_CHUNKS = 8

f32 = jnp.float32
bf16 = jnp.bfloat16


def _rms(x, g):
    return x * lax.rsqrt(jnp.mean(x * x, axis=-1, keepdims=True) + EPS) * g


def _inproj_kernel(x_ref, g_ref, wg_ref, wl_ref, wd_ref, wup_ref, bgk_ref,
                   gq_ref, gk_ref, gv_ref, gog_ref, loga_ref,
                   dq_ref, dk_ref, dv_ref):
    nq = GLA_HEADS * GLA_DK
    nv = GLA_HEADS * GLA_DV
    nd = DIFF_HEADS * 2 * DIFF_DK
    bounds = (0,) + PROJ_ROW_SPLITS + (x_ref.shape[0],)
    rows = [slice(lo, hi) for lo, hi in zip(bounds[:-1], bounds[1:])]
    h = [_rms(x_ref[rw, :], g_ref[...]).astype(bf16) for rw in rows]
    low = [jnp.dot(ha, wl_ref[...], preferred_element_type=f32) for ha in h]
    for rw, lw in zip(rows, low):
        z = jnp.dot(lw.astype(bf16), wup_ref[...], preferred_element_type=f32) + bgk_ref[...]
        log_sig = jnp.minimum(z, 0.0) - jnp.log(1.0 + jnp.exp(-jnp.abs(z)))
        loga_ref[rw, :] = log_sig * (1.0 / GLA_GATE_NORMALIZER)
    for rw, ha in zip(rows, h):
        gla = jnp.dot(ha, wg_ref[...], preferred_element_type=f32)
        gq_ref[rw, :] = gla[:, :nq].astype(bf16)
        gk_ref[rw, :] = gla[:, nq:2 * nq].astype(bf16)
        gv_ref[rw, :] = gla[:, 2 * nq:2 * nq + nv].astype(bf16)
        gog_ref[rw, :] = gla[:, 2 * nq + nv:].astype(bf16)
    for rw, ha in zip(rows, h):
        d = jnp.dot(ha, wd_ref[...], preferred_element_type=f32)
        dq_ref[rw, :] = (d[:, :nd] * (DIFF_DK ** -0.5 * LOG2E)).astype(bf16)
        dk_ref[rw, :] = d[:, nd:2 * nd].astype(bf16)
        dv_ref[rw, :] = d[:, 2 * nd:].astype(bf16)


def _inproj(x2, g, wg, wl, wd, wup, bgk):
    m, d = x2.shape
    tm = TM_INPROJ
    nq = GLA_HEADS * GLA_DK
    nv = GLA_HEADS * GLA_DV
    nd = DIFF_HEADS * 2 * DIFF_DK
    ndv = DIFF_HEADS * DIFF_DV

    def row(n):
        return pl.BlockSpec((tm, n), lambda i: (i, 0))

    def whole(a):
        return pl.BlockSpec(a.shape, lambda i: (0, 0), pipeline_mode=pl.Buffered(1))

    out_shapes = [
        jax.ShapeDtypeStruct((m, nq), bf16), jax.ShapeDtypeStruct((m, nq), bf16),
        jax.ShapeDtypeStruct((m, nv), bf16), jax.ShapeDtypeStruct((m, nv), bf16),
        jax.ShapeDtypeStruct((m, nq), f32),
        jax.ShapeDtypeStruct((m, nd), bf16), jax.ShapeDtypeStruct((m, nd), bf16),
        jax.ShapeDtypeStruct((m, ndv), bf16),
    ]
    return pl.pallas_call(
        _inproj_kernel,
        grid=(m // tm,),
        in_specs=[row(d), whole(g), whole(wg), whole(wl), whole(wd), whole(wup), whole(bgk)],
        out_specs=[row(nq), row(nq), row(nv), row(nv), row(nq), row(nd), row(nd), row(ndv)],
        out_shape=out_shapes,
        compiler_params=pltpu.CompilerParams(
            dimension_semantics=("parallel",), vmem_limit_bytes=VMEM_LIMIT_BYTES),
        name="inproj",
    )(x2, g, wg, wl, wd, wup, bgk)


def _gla_kernel(q_ref, k_ref, v_ref, og_ref, la_ref, g_ref, o_ref, *, seq):
    c = GLA_CHUNK
    row_i = lax.broadcasted_iota(jnp.int32, (c, c), 0)
    col_i = lax.broadcasted_iota(jnp.int32, (c, c), 1)
    tril = row_i >= col_i
    tril_ones = jnp.where(tril, 1.0, 0.0).astype(bf16)
    lane = lax.broadcasted_iota(jnp.int32, (c, LANES), 1)
    head0_lanes = lane < GLA_DK
    srow = lax.broadcasted_iota(jnp.int32, (2 * GLA_DV, LANES), 0)
    slane = lax.broadcasted_iota(jnp.int32, (2 * GLA_DV, LANES), 1)
    state_mask = (srow < GLA_DV) == (slane < GLA_DK)
    g = g_ref[...]

    nt = (((1,), (1,)), ((), ()))
    tn = (((0,), (0,)), ((), ()))
    nb = GLA_BLOCK_CHUNKS
    n_blocks = seq // (nb * c)

    def stage_a(blk):
        rows = [slice((blk * nb + n) * c, (blk * nb + n + 1) * c) for n in range(nb)]
        cum = []
        for n in range(nb):
            la = la_ref[rows[n], :]
            la_hi = la.astype(bf16)
            la_lo = (la - la_hi.astype(f32)).astype(bf16)
            cum.append(jnp.dot(tril_ones, la_hi, preferred_element_type=f32)
                       + jnp.dot(tril_ones, la_lo, preferred_element_type=f32))
        return rows, cum

    def stage_b(rows, cum):
        q_heads, k_intra, k_state, q_inter, decay = [], [], [], [], []
        for n in range(nb):
            b = cum[n]
            b_last = b[c - 1:c, :]
            b_mid = b[c // 2 - 1:c // 2, :]
            q = q_ref[rows[n], :].astype(f32) * (GLA_DK ** -0.5)
            k = k_ref[rows[n], :].astype(f32)
            qi = (q * jnp.exp(b - b_mid)).astype(bf16)
            zero = jnp.zeros_like(qi)
            q_heads.append(jnp.concatenate([jnp.where(head0_lanes, qi, zero),
                                            jnp.where(head0_lanes, zero, qi)], axis=0))
            k_intra.append((k * jnp.exp(b_mid - b)).astype(bf16))
            k_state.append((k * jnp.exp(b_last - b)).astype(bf16))
            q_inter.append((q * jnp.exp(b)).astype(bf16))
            decay.append(jnp.exp(b_last))
        scores = [lax.dot_general(q_heads[n], k_intra[n], nt, preferred_element_type=f32)
                  for n in range(nb)]
        u_t = [lax.dot_general(v_ref[rows[n], :], k_state[n], tn, preferred_element_type=f32)
               for n in range(nb)]
        return scores, u_t, q_inter, decay

    def stage_c(rows, scores, u_t, q_inter, decay, state_t):
        states = []
        for n in range(nb):
            states.append(state_t.astype(bf16))
            state_t = state_t * decay[n] + jnp.where(state_mask, u_t[n], 0.0)
        o_inter = [lax.dot_general(q_inter[n], states[n], nt, preferred_element_type=f32)
                   for n in range(nb)]
        for n in range(nb):
            outs = []
            for hh in range(2):
                cols = slice(hh * GLA_DV, (hh + 1) * GLA_DV)
                s_h = jnp.where(tril, scores[n][hh * c:(hh + 1) * c, :], 0.0).astype(bf16)
                o_h = o_inter[n][:, cols] + jnp.dot(s_h, v_ref[rows[n], cols],
                                                    preferred_element_type=f32)
                og_h = og_ref[rows[n], cols].astype(f32)
                silu = og_h / (1.0 + jnp.exp(-og_h))
                outs.append((_rms(o_h, g) * silu).astype(bf16))
            o_ref[rows[n], :] = jnp.concatenate(outs, axis=1)
        return state_t

    state_t = jnp.zeros((2 * GLA_DV, LANES), f32)
    rows, cum = stage_a(0)
    for blk in range(n_blocks):
        mid = stage_b(rows, cum)
        cur_rows = rows
        if blk + 1 < n_blocks:
            rows, cum = stage_a(blk + 1)
        state_t = stage_c(cur_rows, *mid, state_t)


def _gla(gq, gk, gv, gog, loga, g_head, batch, seq):
    m = gq.shape[0]
    pairs = GLA_HEADS // 2
    kern = functools.partial(_gla_kernel, seq=seq)
    return pl.pallas_call(
        kern,
        grid=(batch, pairs),
        in_specs=[
            pl.BlockSpec((seq, LANES), lambda b, p: (b, p)),
            pl.BlockSpec((seq, LANES), lambda b, p: (b, p)),
            pl.BlockSpec((seq, 2 * GLA_DV), lambda b, p: (b, p)),
            pl.BlockSpec((seq, 2 * GLA_DV), lambda b, p: (b, p)),
            pl.BlockSpec((seq, LANES), lambda b, p: (b, p)),
            pl.BlockSpec((1, GLA_DV), lambda b, p: (0, 0)),
        ],
        out_specs=pl.BlockSpec((seq, 2 * GLA_DV), lambda b, p: (b, p)),
        out_shape=jax.ShapeDtypeStruct((m, GLA_HEADS * GLA_DV), bf16),
        compiler_params=pltpu.CompilerParams(
            dimension_semantics=("parallel", "parallel"), vmem_limit_bytes=VMEM_LIMIT_BYTES),
        name="gla",
    )(gq, gk, gv, gog, loga, g_head)


def _diff_kernel(q_ref, k_ref, v_ref, bias_ref, lq1_ref, lk1_ref, lq2_ref, lk2_ref, g_ref,
                 o_ref, vext_ref, *, lambda_init, seq):
    t = T_ATTN
    nq = seq // t
    heads = ATTN_HEADS_PER_STEP
    for hd in range(heads):
        vext_ref[hd, :, :DIFF_DV] = v_ref[:, hd * DIFF_DV:(hd + 1) * DIFF_DV]
        vext_ref[hd, :, DIFF_DV:] = jnp.ones((seq, DIFF_DV), bf16)
    lane = lax.broadcasted_iota(jnp.int32, (t, LANES), 1)
    map1_lanes = lane < DIFF_DK
    lam = (jnp.exp(jnp.sum(lq1_ref[...] * lk1_ref[...], axis=-1, keepdims=True))
           - jnp.exp(jnp.sum(lq2_ref[...] * lk2_ref[...], axis=-1, keepdims=True))
           + lambda_init)
    g = g_ref[...]
    nt = (((1,), (1,)), ((), ()))

    def key_ranges(i):
        pieces = []
        if i >= 2:
            pieces.append((0, (i - 1) * t, None))
        if i >= 1:
            pieces.append(((i - 1) * t, i * t, 1))
        pieces.append((i * t, (i + 1) * t, 0))
        return pieces

    def qk(hd, i):
        cols = slice(hd * LANES, (hd + 1) * LANES)
        q = q_ref[i * t:(i + 1) * t, cols]
        zero = jnp.zeros_like(q)
        qs = jnp.concatenate([jnp.where(map1_lanes, q, zero),
                              jnp.where(map1_lanes, zero, q)], axis=0)
        logits = []
        for lo, hi, bias in key_ranges(i):
            s = lax.dot_general(qs, k_ref[lo:hi, cols], nt, preferred_element_type=f32)
            if bias is not None:
                tile = bias_ref[hd, bias]
                s = s + jnp.concatenate([tile, tile], axis=0)
            logits.append(s)
        return logits

    units = [(hd, i) for hd in range(heads) for i in range(nq)]
    logits = qk(*units[0])
    for n, (hd, i) in enumerate(units):
        nxt = qk(*units[n + 1]) if n + 1 < len(units) else None
        m = functools.reduce(jnp.maximum, [jnp.max(s, axis=-1, keepdims=True) for s in logits])
        pv = None
        for (lo, hi, _), s in zip(key_ranges(i), logits):
            p = jnp.exp2(s - m).astype(bf16)
            part = jnp.dot(p, vext_ref[hd, lo:hi, :], preferred_element_type=f32)
            pv = part if pv is None else pv + part
        o = pv[:, :DIFF_DV] / pv[:, DIFF_DV:]
        o = o[:t, :] - lam * o[t:, :]
        o_ref[i * t:(i + 1) * t, hd * DIFF_DV:(hd + 1) * DIFF_DV] = (
            _rms(o, g) * (1.0 - lambda_init)).astype(bf16)
        logits = nxt


def _diff_attn(dq, dk, dv, bias_tiles, lq1, lk1, lq2, lk2, g_head, batch, seq, lambda_init):
    m = dq.shape[0]
    t = T_ATTN
    heads = ATTN_HEADS_PER_STEP
    kern = functools.partial(_diff_kernel, lambda_init=lambda_init, seq=seq)
    vec = pl.BlockSpec((1, DIFF_DK), lambda b, h: (0, 0))
    head_rows = pl.BlockSpec((seq, heads * LANES), lambda b, h: (b, h))
    return pl.pallas_call(
        kern,
        grid=(batch, DIFF_HEADS // heads),
        in_specs=[
            head_rows, head_rows, head_rows,
            pl.BlockSpec((heads, 2, t, t), lambda b, h: (h, 0, 0, 0)),
            vec, vec, vec, vec,
            pl.BlockSpec((1, DIFF_DV), lambda b, h: (0, 0)),
        ],
        out_specs=head_rows,
        out_shape=jax.ShapeDtypeStruct((m, DIFF_HEADS * DIFF_DV), bf16),
        scratch_shapes=[pltpu.VMEM((heads, seq, 2 * DIFF_DV), bf16)],
        compiler_params=pltpu.CompilerParams(
            dimension_semantics=("parallel", "parallel"),
            vmem_limit_bytes=VMEM_LIMIT_BYTES),
        name="diff_attn",
    )(dq, dk, dv, bias_tiles, lq1, lk1, lq2, lk2, g_head)


def _toeplitz(g, t):
    lead = g.shape[:-1]
    w = jnp.concatenate([g[..., 1:t + 1][..., ::-1], g[..., :1], g[..., t + 1:][..., ::-1]],
                        axis=-1)
    skew = jnp.broadcast_to(w[..., None, :], lead + (t, 2 * t)).reshape(lead + (-1,))
    return skew[..., :t * (2 * t - 1)].reshape(lead + (t, 2 * t - 1))[..., :t]


def _t5_bias_tiles(rel_bias, t):
    rel = jnp.arange(2 * t)
    max_exact = REL_BUCKETS // 2
    nf = jnp.maximum(rel, 1).astype(f32)
    large = max_exact + (jnp.log(nf / max_exact) / math.log(REL_MAX_DIST / max_exact)
                         * (REL_BUCKETS - max_exact)).astype(jnp.int32)
    large = jnp.minimum(large, REL_BUCKETS - 1)
    bucket = jnp.where(rel < max_exact, rel, large)
    table = rel_bias.astype(f32)
    by_dist = ((table[bucket] - table[REL_BUCKETS - 1][None, :]) * LOG2E).T
    masked = jnp.full((DIFF_HEADS, t), NEG_BIG, f32)
    diag = jnp.concatenate([masked, by_dist[:, :t]], axis=-1)
    return _toeplitz(jnp.stack([diag, by_dist], axis=1), t)


def _outffn_kernel(x_ref, gla_ref, diff_ref, wo_g_ref, wo_d_ref, post_mix_ref, pre_ffn_ref,
                   wgate_ref, wup_ref, wdown_ref, post_ffn_ref, o_ref):
    bounds = (0,) + FFN_ROW_SPLITS + (x_ref.shape[0],)
    rows = [slice(lo, hi) for lo, hi in zip(bounds[:-1], bounds[1:])]
    pairs = [rows[a:a + 2] for a in range(0, len(rows), 2)]

    def out_proj(pair):
        return [jnp.dot(gla_ref[rw, :], wo_g_ref[...], preferred_element_type=f32)
                + jnp.dot(diff_ref[rw, :], wo_d_ref[...], preferred_element_type=f32)
                for rw in pair]

    mix = out_proj(pairs[0])
    for n, pair in enumerate(pairs):
        x1 = [x_ref[rw, :] + _rms(mx, post_mix_ref[...]) for rw, mx in zip(pair, mix)]
        h = [_rms(xa, pre_ffn_ref[...]).astype(bf16) for xa in x1]
        gate_up = [(jnp.dot(ha, wgate_ref[...], preferred_element_type=f32),
                    jnp.dot(ha, wup_ref[...], preferred_element_type=f32)) for ha in h]
        if n + 1 < len(pairs):
            mix = out_proj(pairs[n + 1])
        f = [(gate / (1.0 + jnp.exp(-gate)) * up).astype(bf16) for gate, up in gate_up]
        y = [jnp.dot(fa, wdown_ref[...], preferred_element_type=f32) for fa in f]
        for rw, xa, ya in zip(pair, x1, y):
            o_ref[rw, :] = xa + _rms(ya, post_ffn_ref[...])


def _outffn(x2, gla_o, diff_o, wo_g, wo_d, post_mix, pre_ffn, wgate, wup, wdown, post_ffn):
    m, d = x2.shape
    tm = TM_FFN

    def row(n):
        return pl.BlockSpec((tm, n), lambda i: (i, 0))

    def whole(a):
        return pl.BlockSpec(a.shape, lambda i: (0, 0), pipeline_mode=pl.Buffered(1))

    return pl.pallas_call(
        _outffn_kernel,
        grid=(m // tm,),
        in_specs=[row(d), row(gla_o.shape[1]), row(diff_o.shape[1]),
                  whole(wo_g), whole(wo_d), whole(post_mix), whole(pre_ffn),
                  whole(wgate), whole(wup), whole(wdown), whole(post_ffn)],
        out_specs=row(d),
        out_shape=jax.ShapeDtypeStruct((m, d), f32),
        compiler_params=pltpu.CompilerParams(
            dimension_semantics=("parallel",), vmem_limit_bytes=VMEM_LIMIT_BYTES),
        name="outffn",
    )(x2, gla_o, diff_o, wo_g, wo_d, post_mix, pre_ffn, wgate, wup, wdown, post_ffn)


def kernel(x, w_in, w_gk_up, b_gk, lambda_q1, lambda_k1, lambda_q2, lambda_k2, rel_bias,
           g_gla_head, g_diff_head, w_out, pre_mix_g, post_mix_g, w_gate, w_up, w_down,
           pre_ffn_g, post_ffn_g):
    batch, seq, d = x.shape
    depth = w_in.shape[0]
    assert seq % T_ATTN == 0 and T_ATTN >= REL_MAX_DIST
    assert (batch * seq) % TM_INPROJ == 0 and (batch * seq) % TM_FFN == 0
    assert seq % (GLA_BLOCK_CHUNKS * GLA_CHUNK) == 0
    n_gla = 2 * GLA_HEADS * GLA_DK + 2 * GLA_HEADS * GLA_DV
    bias_tiles = _t5_bias_tiles(rel_bias, T_ATTN)
    x2 = x.reshape(batch * seq, d)
    for l in range(depth):
        lambda_init = 0.8 - 0.6 * math.exp(-0.3 * l)
        wg = w_in[l, :, :n_gla].astype(bf16)
        wl = jnp.pad(w_in[l, :, n_gla:n_gla + GLA_GATE_RANK],
                     ((0, 0), (0, LANES - GLA_GATE_RANK))).astype(bf16)
        wd = w_in[l, :, n_gla + GLA_GATE_RANK:].astype(bf16)
        wup = jnp.pad(w_gk_up[l], ((0, LANES - GLA_GATE_RANK), (0, 0))).astype(bf16)
        gq, gk, gv, gog, loga, dq, dk, dv = _inproj(
            x2, pre_mix_g[l][None, :], wg, wl, wd, wup, b_gk[l][None, :])
        gla_o = _gla(gq, gk, gv, gog, loga, g_gla_head[l][None, :], batch, seq)
        diff_o = _diff_attn(dq, dk, dv, bias_tiles,
                            lambda_q1[l][None, :], lambda_k1[l][None, :],
                            lambda_q2[l][None, :], lambda_k2[l][None, :],
                            g_diff_head[l][None, :], batch, seq, lambda_init)
        n_go = GLA_HEADS * GLA_DV
        x2 = _outffn(x2, gla_o, diff_o,
                     w_out[l, :n_go].astype(bf16), w_out[l, n_go:].astype(bf16),
                     post_mix_g[l][None, :], pre_ffn_g[l][None, :],
                     w_gate[l].astype(bf16), w_up[l].astype(bf16), w_down[l].astype(bf16),
                     post_ffn_g[l][None, :])
    return x2.reshape(batch, seq, d)
```

```python
import functools
import math

import jax
import jax.numpy as jnp
from jax import lax
from jax.experimental import pallas as pl
from jax.experimental.pallas import tpu as pltpu

GLA_HEADS = 4
GLA_DK = 64
GLA_DV = 128
GLA_GATE_RANK = 16
GLA_GATE_NORMALIZER = 16.0
GLA_CHUNK = 64
DIFF_HEADS = 4
DIFF_DK = 64
DIFF_DV = 128
REL_BUCKETS = 32
REL_MAX_DIST = 128
EPS = 1e-6

LANES = 128
LOG2E = 1.4426950408889634
NEG_BIG = -1e30
VMEM_LIMIT_BYTES = 56 * 1024 * 1024

TM_INPROJ = 1024
TM_FFN = 1024
T_ATTN = 256
ATTN_HEADS_PER_STEP = 4
PROJ_ROW_SPLITS = (256, 512, 768)
FFN_ROW_SPLITS = (256, 512, 768)
GLA_BLOCK_CHUNKS = 8

f32 = jnp.float32
bf16 = jnp.bfloat16


def _rms(x, g):
    return x * lax.rsqrt(jnp.mean(x * x, axis=-1, keepdims=True) + EPS) * g


def _inproj_kernel(x_ref, g_ref, wg_ref, wl_ref, wd_ref, wup_ref, bgk_ref,
                   gq_ref, gk_ref, gv_ref, gog_ref, loga_ref,
                   dq_ref, dk_ref, dv_ref):
    nq = GLA_HEADS * GLA_DK
    nv = GLA_HEADS * GLA_DV
    nd = DIFF_HEADS * 2 * DIFF_DK
    bounds = (0,) + PROJ_ROW_SPLITS + (x_ref.shape[0],)
    rows = [slice(lo, hi) for lo, hi in zip(bounds[:-1], bounds[1:])]
    h = [_rms(x_ref[rw, :], g_ref[...]).astype(bf16) for rw in rows]
    low = [jnp.dot(ha, wl_ref[...], preferred_element_type=f32) for ha in h]
    for rw, lw in zip(rows, low):
        z = jnp.dot(lw.astype(bf16), wup_ref[...], preferred_element_type=f32) + bgk_ref[...]
        log_sig = jnp.minimum(z, 0.0) - jnp.log(1.0 + jnp.exp(-jnp.abs(z)))
        loga_ref[rw, :] = log_sig * (LOG2E / GLA_GATE_NORMALIZER)
    for rw, ha in zip(rows, h):
        gla = jnp.dot(ha, wg_ref[...], preferred_element_type=f32)
        gq_ref[rw, :] = (gla[:, :nq] * (GLA_DK ** -0.5)).astype(bf16)
        gk_ref[rw, :] = gla[:, nq:2 * nq].astype(bf16)
        gv_ref[rw, :] = gla[:, 2 * nq:2 * nq + nv].astype(bf16)
        gog_ref[rw, :] = gla[:, 2 * nq + nv:].astype(bf16)
    for rw, ha in zip(rows, h):
        d = jnp.dot(ha, wd_ref[...], preferred_element_type=f32)
        dq_ref[rw, :] = (d[:, :nd] * (DIFF_DK ** -0.5 * LOG2E)).astype(bf16)
        dk_ref[rw, :] = d[:, nd:2 * nd].astype(bf16)
        dv_ref[rw, :] = d[:, 2 * nd:].astype(bf16)


def _inproj(x2, g, wg, wl, wd, wup, bgk):
    m, d = x2.shape
    tm = TM_INPROJ
    nq = GLA_HEADS * GLA_DK
    nv = GLA_HEADS * GLA_DV
    nd = DIFF_HEADS * 2 * DIFF_DK
    ndv = DIFF_HEADS * DIFF_DV

    def row(n):
        return pl.BlockSpec((tm, n), lambda i: (i, 0))

    def whole(a):
        return pl.BlockSpec(a.shape, lambda i: (0, 0), pipeline_mode=pl.Buffered(1))

    out_shapes = [
        jax.ShapeDtypeStruct((m, nq), bf16), jax.ShapeDtypeStruct((m, nq), bf16),
        jax.ShapeDtypeStruct((m, nv), bf16), jax.ShapeDtypeStruct((m, nv), bf16),
        jax.ShapeDtypeStruct((m, nq), f32),
        jax.ShapeDtypeStruct((m, nd), bf16), jax.ShapeDtypeStruct((m, nd), bf16),
        jax.ShapeDtypeStruct((m, ndv), bf16),
    ]
    return pl.pallas_call(
        _inproj_kernel,
        grid=(m // tm,),
        in_specs=[row(d), whole(g), whole(wg), whole(wl), whole(wd), whole(wup), whole(bgk)],
        out_specs=[row(nq), row(nq), row(nv), row(nv), row(nq), row(nd), row(nd), row(ndv)],
        out_shape=out_shapes,
        compiler_params=pltpu.CompilerParams(
            dimension_semantics=("parallel",), vmem_limit_bytes=VMEM_LIMIT_BYTES),
        name="inproj",
    )(x2, g, wg, wl, wd, wup, bgk)


def _gla_kernel(q_ref, k_ref, v_ref, og_ref, la_ref, g_ref, o_ref, *, seq):
    c = GLA_CHUNK
    row_i = lax.broadcasted_iota(jnp.int32, (c, c), 0)
    col_i = lax.broadcasted_iota(jnp.int32, (c, c), 1)
    tril = row_i >= col_i
    tril_ones = jnp.where(tril, 1.0, 0.0).astype(bf16)
    lane = lax.broadcasted_iota(jnp.int32, (c, LANES), 1)
    head0_lanes = lane < GLA_DK
    srow = lax.broadcasted_iota(jnp.int32, (2 * GLA_DV, LANES), 0)
    slane = lax.broadcasted_iota(jnp.int32, (2 * GLA_DV, LANES), 1)
    state_mask = (srow < GLA_DV) == (slane < GLA_DK)
    g = g_ref[...]

    nt = (((1,), (1,)), ((), ()))
    tn = (((0,), (0,)), ((), ()))
    nb = GLA_BLOCK_CHUNKS
    n_blocks = seq // (nb * c)

    def stage_a(blk):
        rows = [slice((blk * nb + n) * c, (blk * nb + n + 1) * c) for n in range(nb)]
        cum = []
        for n in range(nb):
            la = la_ref[rows[n], :]
            la_hi = la.astype(bf16)
            la_lo = (la - la_hi.astype(f32)).astype(bf16)
            cum.append(jnp.dot(tril_ones, la_hi, preferred_element_type=f32)
                       + jnp.dot(tril_ones, la_lo, preferred_element_type=f32))
        return rows, cum

    def stage_b(rows, cum):
        q_heads, k_intra, k_state, q_inter, decay = [], [], [], [], []
        for n in range(nb):
            b = cum[n]
            b_last = b[c - 1:c, :]
            b_mid = b[c // 2 - 1:c // 2, :]
            q = q_ref[rows[n], :].astype(f32)
            k = k_ref[rows[n], :].astype(f32)
            qi = (q * jnp.exp2(b - b_mid)).astype(bf16)
            zero = jnp.zeros_like(qi)
            q_heads.append(jnp.concatenate([jnp.where(head0_lanes, qi, zero),
                                            jnp.where(head0_lanes, zero, qi)], axis=0))
            k_intra.append((k * jnp.exp2(b_mid - b)).astype(bf16))
            k_state.append((k * jnp.exp2(b_last - b)).astype(bf16))
            q_inter.append((q * jnp.exp2(b)).astype(bf16))
            decay.append(jnp.exp2(b_last))
        scores = [lax.dot_general(q_heads[n], k_intra[n], nt, preferred_element_type=f32)
                  for n in range(nb)]
        u_t = [lax.dot_general(v_ref[rows[n], :], k_state[n], tn, preferred_element_type=f32)
               for n in range(nb)]
        return scores, u_t, q_inter, decay

    def stage_c(rows, scores, u_t, q_inter, decay, state_t):
        states = []
        for n in range(nb):
            states.append(jnp.where(state_mask, state_t.astype(bf16), jnp.zeros((), bf16)))
            state_t = state_t * decay[n] + u_t[n]
        o_inter = [lax.dot_general(q_inter[n], states[n], nt, preferred_element_type=f32)
                   for n in range(nb)]
        for n in range(nb):
            outs = []
            for hh in range(2):
                cols = slice(hh * GLA_DV, (hh + 1) * GLA_DV)
                s_h = jnp.where(tril, scores[n][hh * c:(hh + 1) * c, :], 0.0).astype(bf16)
                o_h = o_inter[n][:, cols] + jnp.dot(s_h, v_ref[rows[n], cols],
                                                    preferred_element_type=f32)
                og_h = og_ref[rows[n], cols].astype(f32)
                silu = og_h / (1.0 + jnp.exp(-og_h))
                outs.append((_rms(o_h, g) * silu).astype(bf16))
            o_ref[rows[n], :] = jnp.concatenate(outs, axis=1)
        return state_t

    state_t = jnp.zeros((2 * GLA_DV, LANES), f32)
    rows, cum = stage_a(0)
    for blk in range(n_blocks):
        mid = stage_b(rows, cum)
        cur_rows = rows
        if blk + 1 < n_blocks:
            rows, cum = stage_a(blk + 1)
        state_t = stage_c(cur_rows, *mid, state_t)


def _gla(gq, gk, gv, gog, loga, g_head, batch, seq):
    m = gq.shape[0]
    pairs = GLA_HEADS // 2
    kern = functools.partial(_gla_kernel, seq=seq)
    return pl.pallas_call(
        kern,
        grid=(batch, pairs),
        in_specs=[
            pl.BlockSpec((seq, LANES), lambda b, p: (b, p)),
            pl.BlockSpec((seq, LANES), lambda b, p: (b, p)),
            pl.BlockSpec((seq, 2 * GLA_DV), lambda b, p: (b, p)),
            pl.BlockSpec((seq, 2 * GLA_DV), lambda b, p: (b, p)),
            pl.BlockSpec((seq, LANES), lambda b, p: (b, p)),
            pl.BlockSpec((1, GLA_DV), lambda b, p: (0, 0)),
        ],
        out_specs=pl.BlockSpec((seq, 2 * GLA_DV), lambda b, p: (b, p)),
        out_shape=jax.ShapeDtypeStruct((m, GLA_HEADS * GLA_DV), bf16),
        compiler_params=pltpu.CompilerParams(
            dimension_semantics=("parallel", "parallel"), vmem_limit_bytes=VMEM_LIMIT_BYTES),
        name="gla",
    )(gq, gk, gv, gog, loga, g_head)


def _diff_kernel(q_ref, k_ref, v_ref, bias_ref, lq1_ref, lk1_ref, lq2_ref, lk2_ref, g_ref,
                 o_ref, vext_ref, *, lambda_init, seq):
    t = T_ATTN
    nq = seq // t
    heads = ATTN_HEADS_PER_STEP
    for hd in range(heads):
        vext_ref[hd, :, :DIFF_DV] = v_ref[:, hd * DIFF_DV:(hd + 1) * DIFF_DV]
        vext_ref[hd, :, DIFF_DV:] = jnp.ones((seq, DIFF_DV), bf16)
    lane = lax.broadcasted_iota(jnp.int32, (t, LANES), 1)
    map1_lanes = lane < DIFF_DK
    lam = (jnp.exp(jnp.sum(lq1_ref[...] * lk1_ref[...], axis=-1, keepdims=True))
           - jnp.exp(jnp.sum(lq2_ref[...] * lk2_ref[...], axis=-1, keepdims=True))
           + lambda_init)
    g = g_ref[...]
    nt = (((1,), (1,)), ((), ()))

    def key_ranges(i):
        pieces = []
        if i >= 2:
            pieces.append((0, (i - 1) * t, None))
        if i >= 1:
            pieces.append(((i - 1) * t, i * t, 1))
        pieces.append((i * t, (i + 1) * t, 0))
        return pieces

    def qk(hd, i):
        cols = slice(hd * LANES, (hd + 1) * LANES)
        q = q_ref[i * t:(i + 1) * t, cols]
        zero = jnp.zeros_like(q)
        qs = jnp.concatenate([jnp.where(map1_lanes, q, zero),
                              jnp.where(map1_lanes, zero, q)], axis=0)
        logits = []
        for lo, hi, bias in key_ranges(i):
            s = lax.dot_general(qs, k_ref[lo:hi, cols], nt, preferred_element_type=f32)
            if bias is not None:
                tile = bias_ref[hd, bias]
                s = s + jnp.concatenate([tile, tile], axis=0)
            logits.append(s)
        return logits

    units = [(hd, i) for hd in range(heads) for i in range(nq)]
    logits = qk(*units[0])
    for n, (hd, i) in enumerate(units):
        nxt = qk(*units[n + 1]) if n + 1 < len(units) else None
        m = functools.reduce(jnp.maximum, [jnp.max(s, axis=-1, keepdims=True) for s in logits])
        pv = None
        for (lo, hi, _), s in zip(key_ranges(i), logits):
            p = jnp.exp2(s - m).astype(bf16)
            part = jnp.dot(p, vext_ref[hd, lo:hi, :], preferred_element_type=f32)
            pv = part if pv is None else pv + part
        o = pv[:, :DIFF_DV] / pv[:, DIFF_DV:]
        o = o[:t, :] - lam * o[t:, :]
        o_ref[i * t:(i + 1) * t, hd * DIFF_DV:(hd + 1) * DIFF_DV] = (
            _rms(o, g) * (1.0 - lambda_init)).astype(bf16)
        logits = nxt


def _diff_attn(dq, dk, dv, bias_tiles, lq1, lk1, lq2, lk2, g_head, batch, seq, lambda_init):
    m = dq.shape[0]
    t = T_ATTN
    heads = ATTN_HEADS_PER_STEP
    kern = functools.partial(_diff_kernel, lambda_init=lambda_init, seq=seq)
    vec = pl.BlockSpec((1, DIFF_DK), lambda b, h: (0, 0))
    head_rows = pl.BlockSpec((seq, heads * LANES), lambda b, h: (b, h))
    return pl.pallas_call(
        kern,
        grid=(batch, DIFF_HEADS // heads),
        in_specs=[
            head_rows, head_rows, head_rows,
            pl.BlockSpec((heads, 2, t, t), lambda b, h: (h, 0, 0, 0)),
            vec, vec, vec, vec,
            pl.BlockSpec((1, DIFF_DV), lambda b, h: (0, 0)),
        ],
        out_specs=head_rows,
        out_shape=jax.ShapeDtypeStruct((m, DIFF_HEADS * DIFF_DV), bf16),
        scratch_shapes=[pltpu.VMEM((heads, seq, 2 * DIFF_DV), bf16)],
        compiler_params=pltpu.CompilerParams(
            dimension_semantics=("parallel", "parallel"),
            vmem_limit_bytes=VMEM_LIMIT_BYTES),
        name="diff_attn",
    )(dq, dk, dv, bias_tiles, lq1, lk1, lq2, lk2, g_head)


def _toeplitz(g, t):
    lead = g.shape[:-1]
    w = jnp.concatenate([g[..., 1:t + 1][..., ::-1], g[..., :1], g[..., t + 1:][..., ::-1]],
                        axis=-1)
    skew = jnp.broadcast_to(w[..., None, :], lead + (t, 2 * t)).reshape(lead + (-1,))
    return skew[..., :t * (2 * t - 1)].reshape(lead + (t, 2 * t - 1))[..., :t]


def _t5_bias_tiles(rel_bias, t):
    rel = jnp.arange(2 * t)
    max_exact = REL_BUCKETS // 2
    nf = jnp.maximum(rel, 1).astype(f32)
    large = max_exact + (jnp.log(nf / max_exact) / math.log(REL_MAX_DIST / max_exact)
                         * (REL_BUCKETS - max_exact)).astype(jnp.int32)
    large = jnp.minimum(large, REL_BUCKETS - 1)
    bucket = jnp.where(rel < max_exact, rel, large)
    table = rel_bias.astype(f32)
    by_dist = ((table[bucket] - table[REL_BUCKETS - 1][None, :]) * LOG2E).T
    masked = jnp.full((DIFF_HEADS, t), NEG_BIG, f32)
    diag = jnp.concatenate([masked, by_dist[:, :t]], axis=-1)
    return _toeplitz(jnp.stack([diag, by_dist], axis=1), t)


def _outffn_kernel(x_ref, gla_ref, diff_ref, wo_g_ref, wo_d_ref, post_mix_ref, pre_ffn_ref,
                   wgate_ref, wup_ref, wdown_ref, post_ffn_ref, o_ref):
    bounds = (0,) + FFN_ROW_SPLITS + (x_ref.shape[0],)
    rows = [slice(lo, hi) for lo, hi in zip(bounds[:-1], bounds[1:])]
    pairs = [rows[a:a + 2] for a in range(0, len(rows), 2)]

    def out_proj(pair):
        return [jnp.dot(gla_ref[rw, :], wo_g_ref[...], preferred_element_type=f32)
                + jnp.dot(diff_ref[rw, :], wo_d_ref[...], preferred_element_type=f32)
                for rw in pair]

    mix = out_proj(pairs[0])
    for n, pair in enumerate(pairs):
        x1 = [x_ref[rw, :] + _rms(mx, post_mix_ref[...]) for rw, mx in zip(pair, mix)]
        h = [_rms(xa, pre_ffn_ref[...]).astype(bf16) for xa in x1]
        gate_up = [(jnp.dot(ha, wgate_ref[...], preferred_element_type=f32),
                    jnp.dot(ha, wup_ref[...], preferred_element_type=f32)) for ha in h]
        if n + 1 < len(pairs):
            mix = out_proj(pairs[n + 1])
        f = [(gate / (1.0 + jnp.exp(-gate)) * up).astype(bf16) for gate, up in gate_up]
        y = [jnp.dot(fa, wdown_ref[...], preferred_element_type=f32) for fa in f]
        for rw, xa, ya in zip(pair, x1, y):
            o_ref[rw, :] = xa + _rms(ya, post_ffn_ref[...])


def _outffn(x2, gla_o, diff_o, wo_g, wo_d, post_mix, pre_ffn, wgate, wup, wdown, post_ffn):
    m, d = x2.shape
    tm = TM_FFN

    def row(n):
        return pl.BlockSpec((tm, n), lambda i: (i, 0))

    def whole(a):
        return pl.BlockSpec(a.shape, lambda i: (0, 0), pipeline_mode=pl.Buffered(1))

    return pl.pallas_call(
        _outffn_kernel,
        grid=(m // tm,),
        in_specs=[row(d), row(gla_o.shape[1]), row(diff_o.shape[1]),
                  whole(wo_g), whole(wo_d), whole(post_mix), whole(pre_ffn),
                  whole(wgate), whole(wup), whole(wdown), whole(post_ffn)],
        out_specs=row(d),
        out_shape=jax.ShapeDtypeStruct((m, d), f32),
        compiler_params=pltpu.CompilerParams(
            dimension_semantics=("parallel",), vmem_limit_bytes=VMEM_LIMIT_BYTES),
        name="outffn",
    )(x2, gla_o, diff_o, wo_g, wo_d, post_mix, pre_ffn, wgate, wup, wdown, post_ffn)


def kernel(x, w_in, w_gk_up, b_gk, lambda_q1, lambda_k1, lambda_q2, lambda_k2, rel_bias,
           g_gla_head, g_diff_head, w_out, pre_mix_g, post_mix_g, w_gate, w_up, w_down,
           pre_ffn_g, post_ffn_g):
    batch, seq, d = x.shape
    depth = w_in.shape[0]
    assert seq % T_ATTN == 0 and T_ATTN >= REL_MAX_DIST
    assert (batch * seq) % TM_INPROJ == 0 and (batch * seq) % TM_FFN == 0
    assert seq % (GLA_BLOCK_CHUNKS * GLA_CHUNK) == 0
    n_gla = 2 * GLA_HEADS * GLA_DK + 2 * GLA_HEADS * GLA_DV
    bias_tiles = _t5_bias_tiles(rel_bias, T_ATTN)
    x2 = x.reshape(batch * seq, d)
    for l in range(depth):
        lambda_init = 0.8 - 0.6 * math.exp(-0.3 * l)
        wg = w_in[l, :, :n_gla].astype(bf16)
        wl = jnp.pad(w_in[l, :, n_gla:n_gla + GLA_GATE_RANK],
                     ((0, 0), (0, LANES - GLA_GATE_RANK))).astype(bf16)
        wd = w_in[l, :, n_gla + GLA_GATE_RANK:].astype(bf16)
        wup = jnp.pad(w_gk_up[l], ((0, LANES - GLA_GATE_RANK), (0, 0))).astype(bf16)
        gq, gk, gv, gog, loga, dq, dk, dv = _inproj(
            x2, pre_mix_g[l][None, :], wg, wl, wd, wup, b_gk[l][None, :])
        gla_o = _gla(gq, gk, gv, gog, loga, g_gla_head[l][None, :], batch, seq)
        diff_o = _diff_attn(dq, dk, dv, bias_tiles,
                            lambda_q1[l][None, :], lambda_k1[l][None, :],
                            lambda_q2[l][None, :], lambda_k2[l][None, :],
                            g_diff_head[l][None, :], batch, seq, lambda_init)
        n_go = GLA_HEADS * GLA_DV
        x2 = _outffn(x2, gla_o, diff_o,
                     w_out[l, :n_go].astype(bf16), w_out[l, n_go:].astype(bf16),
                     post_mix_g[l][None, :], pre_ffn_g[l][None, :],
                     w_gate[l].astype(bf16), w_up[l].astype(bf16), w_down[l].astype(bf16),
                     post_ffn_g[l][None, :])
    return x2.reshape(batch, seq, d)
```

```python
import functools
import math

import jax
import jax.numpy as jnp
from jax import lax
from jax.experimental import pallas as pl
from jax.experimental.pallas import tpu as pltpu

GLA_HEADS = 4
GLA_DK = 64
GLA_DV = 128
GLA_GATE_RANK = 16
GLA_GATE_NORMALIZER = 16.0
GLA_CHUNK = 64
DIFF_HEADS = 4
DIFF_DK = 64
DIFF_DV = 128
REL_BUCKETS = 32
REL_MAX_DIST = 128
EPS = 1e-6

LANES = 128
LOG2E = 1.4426950408889634
NEG_BIG = -1e30
VMEM_LIMIT_BYTES = 56 * 1024 * 1024

TM_INPROJ = 1024
TM_FFN = 1024
T_ATTN = 256
ATTN_HEADS_PER_STEP = 2
PROJ_ROW_SPLITS = (256, 512, 768)
FFN_ROW_SPLITS = (256, 512, 768)
GLA_BLOCK_CHUNKS = 8

f32 = jnp.float32
bf16 = jnp.bfloat16


def _rms(x, g):
    return x * lax.rsqrt(jnp.mean(x * x, axis=-1, keepdims=True) + EPS) * g


def _inproj_kernel(x_ref, g_ref, wg_ref, wl_ref, wd_ref, wup_ref, bgk_ref,
                   gq_ref, gk_ref, gv_ref, gog_ref, loga_ref,
                   dq_ref, dk_ref, dv_ref):
    nq = GLA_HEADS * GLA_DK
    nv = GLA_HEADS * GLA_DV
    nd = DIFF_HEADS * 2 * DIFF_DK
    bounds = (0,) + PROJ_ROW_SPLITS + (x_ref.shape[0],)
    rows = [slice(lo, hi) for lo, hi in zip(bounds[:-1], bounds[1:])]
    h = [_rms(x_ref[rw, :], g_ref[...]).astype(bf16) for rw in rows]
    low = [jnp.dot(ha, wl_ref[...], preferred_element_type=f32) for ha in h]
    for rw, lw in zip(rows, low):
        z = jnp.dot(lw.astype(bf16), wup_ref[...], preferred_element_type=f32) + bgk_ref[...]
        log_sig = jnp.minimum(z, 0.0) - jnp.log(1.0 + jnp.exp(-jnp.abs(z)))
        loga_ref[rw, :] = log_sig * (LOG2E / GLA_GATE_NORMALIZER)
    for rw, ha in zip(rows, h):
        gla = jnp.dot(ha, wg_ref[...], preferred_element_type=f32)
        gq_ref[rw, :] = (gla[:, :nq] * (GLA_DK ** -0.5)).astype(bf16)
        gk_ref[rw, :] = gla[:, nq:2 * nq].astype(bf16)
        gv_ref[rw, :] = gla[:, 2 * nq:2 * nq + nv].astype(bf16)
        gog_ref[rw, :] = gla[:, 2 * nq + nv:].astype(bf16)
    for rw, ha in zip(rows, h):
        d = jnp.dot(ha, wd_ref[...], preferred_element_type=f32)
        dq_ref[rw, :] = (d[:, :nd] * (DIFF_DK ** -0.5 * LOG2E)).astype(bf16)
        dk_ref[rw, :] = d[:, nd:2 * nd].astype(bf16)
        dv_ref[rw, :] = d[:, 2 * nd:].astype(bf16)


def _inproj(x2, g, wg, wl, wd, wup, bgk):
    m, d = x2.shape
    tm = TM_INPROJ
    nq = GLA_HEADS * GLA_DK
    nv = GLA_HEADS * GLA_DV
    nd = DIFF_HEADS * 2 * DIFF_DK
    ndv = DIFF_HEADS * DIFF_DV

    def row(n):
        return pl.BlockSpec((tm, n), lambda i: (i, 0))

    def whole(a):
        return pl.BlockSpec(a.shape, lambda i: (0, 0), pipeline_mode=pl.Buffered(1))

    out_shapes = [
        jax.ShapeDtypeStruct((m, nq), bf16), jax.ShapeDtypeStruct((m, nq), bf16),
        jax.ShapeDtypeStruct((m, nv), bf16), jax.ShapeDtypeStruct((m, nv), bf16),
        jax.ShapeDtypeStruct((m, nq), f32),
        jax.ShapeDtypeStruct((m, nd), bf16), jax.ShapeDtypeStruct((m, nd), bf16),
        jax.ShapeDtypeStruct((m, ndv), bf16),
    ]
    return pl.pallas_call(
        _inproj_kernel,
        grid=(m // tm,),
        in_specs=[row(d), whole(g), whole(wg), whole(wl), whole(wd), whole(wup), whole(bgk)],
        out_specs=[row(nq), row(nq), row(nv), row(nv), row(nq), row(nd), row(nd), row(ndv)],
        out_shape=out_shapes,
        compiler_params=pltpu.CompilerParams(
            dimension_semantics=("parallel",), vmem_limit_bytes=VMEM_LIMIT_BYTES),
        name="inproj",
    )(x2, g, wg, wl, wd, wup, bgk)


def _gla_kernel(q_ref, k_ref, v_ref, og_ref, la_ref, g_ref, o_ref, *, seq):
    c = GLA_CHUNK
    row_i = lax.broadcasted_iota(jnp.int32, (c, c), 0)
    col_i = lax.broadcasted_iota(jnp.int32, (c, c), 1)
    tril = row_i >= col_i
    tril_ones = jnp.where(tril, 1.0, 0.0).astype(bf16)
    lane = lax.broadcasted_iota(jnp.int32, (c, LANES), 1)
    head0_lanes = lane < GLA_DK
    srow = lax.broadcasted_iota(jnp.int32, (2 * GLA_DV, LANES), 0)
    slane = lax.broadcasted_iota(jnp.int32, (2 * GLA_DV, LANES), 1)
    state_mask = (srow < GLA_DV) == (slane < GLA_DK)
    g = g_ref[...]

    nt = (((1,), (1,)), ((), ()))
    tn = (((0,), (0,)), ((), ()))
    nb = GLA_BLOCK_CHUNKS
    n_blocks = seq // (nb * c)

    def stage_a(blk):
        rows = [slice((blk * nb + n) * c, (blk * nb + n + 1) * c) for n in range(nb)]
        cum = []
        for n in range(nb):
            la = la_ref[rows[n], :]
            la_hi = la.astype(bf16)
            la_lo = (la - la_hi.astype(f32)).astype(bf16)
            cum.append(jnp.dot(tril_ones, la_hi, preferred_element_type=f32)
                       + jnp.dot(tril_ones, la_lo, preferred_element_type=f32))
        return rows, cum

    def stage_b(rows, cum):
        q_heads, k_intra, k_state, q_inter, decay = [], [], [], [], []
        for n in range(nb):
            b = cum[n]
            b_last = b[c - 1:c, :]
            b_mid = b[c // 2 - 1:c // 2, :]
            q = q_ref[rows[n], :].astype(f32)
            k = k_ref[rows[n], :].astype(f32)
            qi = (q * jnp.exp2(b - b_mid)).astype(bf16)
            zero = jnp.zeros_like(qi)
            q_heads.append(jnp.concatenate([jnp.where(head0_lanes, qi, zero),
                                            jnp.where(head0_lanes, zero, qi)], axis=0))
            k_intra.append((k * jnp.exp2(b_mid - b)).astype(bf16))
            k_state.append((k * jnp.exp2(b_last - b)).astype(bf16))
            q_inter.append((q * jnp.exp2(b)).astype(bf16))
            decay.append(jnp.exp2(b_last))
        scores = [lax.dot_general(q_heads[n], k_intra[n], nt, preferred_element_type=f32)
                  for n in range(nb)]
        u_t = [lax.dot_general(v_ref[rows[n], :], k_state[n], tn, preferred_element_type=f32)
               for n in range(nb)]
        return scores, u_t, q_inter, decay

    def stage_c(rows, scores, u_t, q_inter, decay, state_t):
        states = []
        for n in range(nb):
            states.append(jnp.where(state_mask, state_t.astype(bf16), jnp.zeros((), bf16)))
            state_t = state_t * decay[n] + u_t[n]
        o_inter = [lax.dot_general(q_inter[n], states[n], nt, preferred_element_type=f32)
                   for n in range(nb)]
        for n in range(nb):
            outs = []
            for hh in range(2):
                cols = slice(hh * GLA_DV, (hh + 1) * GLA_DV)
                s_h = jnp.where(tril, scores[n][hh * c:(hh + 1) * c, :], 0.0).astype(bf16)
                o_h = o_inter[n][:, cols] + jnp.dot(s_h, v_ref[rows[n], cols],
                                                    preferred_element_type=f32)
                og_h = og_ref[rows[n], cols].astype(f32)
                silu = og_h / (1.0 + jnp.exp(-og_h))
                outs.append((_rms(o_h, g) * silu).astype(bf16))
            o_ref[rows[n], :] = jnp.concatenate(outs, axis=1)
        return state_t

    state_t = jnp.zeros((2 * GLA_DV, LANES), f32)
    rows, cum = stage_a(0)
    for blk in range(n_blocks):
        mid = stage_b(rows, cum)
        cur_rows = rows
        if blk + 1 < n_blocks:
            rows, cum = stage_a(blk + 1)
        state_t = stage_c(cur_rows, *mid, state_t)


def _gla(gq, gk, gv, gog, loga, g_head, batch, seq):
    m = gq.shape[0]
    pairs = GLA_HEADS // 2
    kern = functools.partial(_gla_kernel, seq=seq)
    return pl.pallas_call(
        kern,
        grid=(batch, pairs),
        in_specs=[
            pl.BlockSpec((seq, LANES), lambda b, p: (b, p)),
            pl.BlockSpec((seq, LANES), lambda b, p: (b, p)),
            pl.BlockSpec((seq, 2 * GLA_DV), lambda b, p: (b, p)),
            pl.BlockSpec((seq, 2 * GLA_DV), lambda b, p: (b, p)),
            pl.BlockSpec((seq, LANES), lambda b, p: (b, p)),
            pl.BlockSpec((1, GLA_DV), lambda b, p: (0, 0)),
        ],
        out_specs=pl.BlockSpec((seq, 2 * GLA_DV), lambda b, p: (b, p)),
        out_shape=jax.ShapeDtypeStruct((m, GLA_HEADS * GLA_DV), bf16),
        compiler_params=pltpu.CompilerParams(
            dimension_semantics=("parallel", "parallel"), vmem_limit_bytes=VMEM_LIMIT_BYTES),
        name="gla",
    )(gq, gk, gv, gog, loga, g_head)


def _diff_kernel(q_ref, k_ref, v_ref, bias_ref, lq1_ref, lk1_ref, lq2_ref, lk2_ref, g_ref,
                 o_ref, vext_ref, *, lambda_init, seq):
    t = T_ATTN
    nq = seq // t
    heads = ATTN_HEADS_PER_STEP
    for hd in range(heads):
        vext_ref[hd, :, :DIFF_DV] = v_ref[:, hd * DIFF_DV:(hd + 1) * DIFF_DV]
        vext_ref[hd, :, DIFF_DV:] = jnp.ones((seq, DIFF_DV), bf16)
    lane = lax.broadcasted_iota(jnp.int32, (t, LANES), 1)
    map1_lanes = lane < DIFF_DK
    lam = (jnp.exp(jnp.sum(lq1_ref[...] * lk1_ref[...], axis=-1, keepdims=True))
           - jnp.exp(jnp.sum(lq2_ref[...] * lk2_ref[...], axis=-1, keepdims=True))
           + lambda_init)
    g = g_ref[...]
    nt = (((1,), (1,)), ((), ()))

    def key_ranges(i):
        pieces = []
        if i >= 2:
            pieces.append((0, (i - 1) * t, None))
        if i >= 1:
            pieces.append(((i - 1) * t, i * t, 1))
        pieces.append((i * t, (i + 1) * t, 0))
        return pieces

    def qk(hd, i):
        cols = slice(hd * LANES, (hd + 1) * LANES)
        q = q_ref[i * t:(i + 1) * t, cols]
        zero = jnp.zeros_like(q)
        qs = jnp.concatenate([jnp.where(map1_lanes, q, zero),
                              jnp.where(map1_lanes, zero, q)], axis=0)
        logits = []
        for lo, hi, bias in key_ranges(i):
            s = lax.dot_general(qs, k_ref[lo:hi, cols], nt, preferred_element_type=f32)
            if bias is not None:
                tile = bias_ref[hd, bias]
                s = s + jnp.concatenate([tile, tile], axis=0)
            logits.append(s)
        return logits

    units = [(hd, i) for hd in range(heads) for i in range(nq)]
    logits = qk(*units[0])
    for n, (hd, i) in enumerate(units):
        nxt = qk(*units[n + 1]) if n + 1 < len(units) else None
        m = functools.reduce(jnp.maximum, [jnp.max(s, axis=-1, keepdims=True) for s in logits])
        pv = None
        for (lo, hi, _), s in zip(key_ranges(i), logits):
            p = jnp.exp2((s - m).astype(bf16))
            part = jnp.dot(p, vext_ref[hd, lo:hi, :], preferred_element_type=f32)
            pv = part if pv is None else pv + part
        o = pv[:, :DIFF_DV] / pv[:, DIFF_DV:]
        o = o[:t, :] - lam * o[t:, :]
        o_ref[i * t:(i + 1) * t, hd * DIFF_DV:(hd + 1) * DIFF_DV] = (
            _rms(o, g) * (1.0 - lambda_init)).astype(bf16)
        logits = nxt


def _diff_attn(dq, dk, dv, bias_tiles, lq1, lk1, lq2, lk2, g_head, batch, seq, lambda_init):
    m = dq.shape[0]
    t = T_ATTN
    heads = ATTN_HEADS_PER_STEP
    kern = functools.partial(_diff_kernel, lambda_init=lambda_init, seq=seq)
    vec = pl.BlockSpec((1, DIFF_DK), lambda b, h: (0, 0))
    head_rows = pl.BlockSpec((seq, heads * LANES), lambda b, h: (b, h))
    return pl.pallas_call(
        kern,
        grid=(batch, DIFF_HEADS // heads),
        in_specs=[
            head_rows, head_rows, head_rows,
            pl.BlockSpec((heads, 2, t, t), lambda b, h: (h, 0, 0, 0)),
            vec, vec, vec, vec,
            pl.BlockSpec((1, DIFF_DV), lambda b, h: (0, 0)),
        ],
        out_specs=head_rows,
        out_shape=jax.ShapeDtypeStruct((m, DIFF_HEADS * DIFF_DV), bf16),
        scratch_shapes=[pltpu.VMEM((heads, seq, 2 * DIFF_DV), bf16)],
        compiler_params=pltpu.CompilerParams(
            dimension_semantics=("parallel", "parallel"),
            vmem_limit_bytes=VMEM_LIMIT_BYTES),
        name="diff_attn",
    )(dq, dk, dv, bias_tiles, lq1, lk1, lq2, lk2, g_head)


def _toeplitz(g, t):
    lead = g.shape[:-1]
    w = jnp.concatenate([g[..., 1:t + 1][..., ::-1], g[..., :1], g[..., t + 1:][..., ::-1]],
                        axis=-1)
    skew = jnp.broadcast_to(w[..., None, :], lead + (t, 2 * t)).reshape(lead + (-1,))
    return skew[..., :t * (2 * t - 1)].reshape(lead + (t, 2 * t - 1))[..., :t]


def _t5_bias_tiles(rel_bias, t):
    rel = jnp.arange(2 * t)
    max_exact = REL_BUCKETS // 2
    nf = jnp.maximum(rel, 1).astype(f32)
    large = max_exact + (jnp.log(nf / max_exact) / math.log(REL_MAX_DIST / max_exact)
                         * (REL_BUCKETS - max_exact)).astype(jnp.int32)
    large = jnp.minimum(large, REL_BUCKETS - 1)
    bucket = jnp.where(rel < max_exact, rel, large)
    table = rel_bias.astype(f32)
    by_dist = ((table[bucket] - table[REL_BUCKETS - 1][None, :]) * LOG2E).T
    masked = jnp.full((DIFF_HEADS, t), NEG_BIG, f32)
    diag = jnp.concatenate([masked, by_dist[:, :t]], axis=-1)
    return _toeplitz(jnp.stack([diag, by_dist], axis=1), t)


def _outffn_kernel(x_ref, gla_ref, diff_ref, wo_g_ref, wo_d_ref, post_mix_ref, pre_ffn_ref,
                   wgate_ref, wup_ref, wdown_ref, post_ffn_ref, o_ref):
    bounds = (0,) + FFN_ROW_SPLITS + (x_ref.shape[0],)
    rows = [slice(lo, hi) for lo, hi in zip(bounds[:-1], bounds[1:])]
    pairs = [rows[a:a + 2] for a in range(0, len(rows), 2)]

    def out_proj(pair):
        return [jnp.dot(gla_ref[rw, :], wo_g_ref[...], preferred_element_type=f32)
                + jnp.dot(diff_ref[rw, :], wo_d_ref[...], preferred_element_type=f32)
                for rw in pair]

    mix = out_proj(pairs[0])
    for n, pair in enumerate(pairs):
        x1 = [x_ref[rw, :] + _rms(mx, post_mix_ref[...]) for rw, mx in zip(pair, mix)]
        h = [_rms(xa, pre_ffn_ref[...]).astype(bf16) for xa in x1]
        gate_up = [(jnp.dot(ha, wgate_ref[...], preferred_element_type=f32),
                    jnp.dot(ha, wup_ref[...], preferred_element_type=f32)) for ha in h]
        if n + 1 < len(pairs):
            mix = out_proj(pairs[n + 1])
        f = [(gate / (1.0 + jnp.exp(-gate)) * up).astype(bf16) for gate, up in gate_up]
        y = [jnp.dot(fa, wdown_ref[...], preferred_element_type=f32) for fa in f]
        for rw, xa, ya in zip(pair, x1, y):
            o_ref[rw, :] = xa + _rms(ya, post_ffn_ref[...])


def _outffn(x2, gla_o, diff_o, wo_g, wo_d, post_mix, pre_ffn, wgate, wup, wdown, post_ffn):
    m, d = x2.shape
    tm = TM_FFN

    def row(n):
        return pl.BlockSpec((tm, n), lambda i: (i, 0))

    def whole(a):
        return pl.BlockSpec(a.shape, lambda i: (0, 0), pipeline_mode=pl.Buffered(1))

    return pl.pallas_call(
        _outffn_kernel,
        grid=(m // tm,),
        in_specs=[row(d), row(gla_o.shape[1]), row(diff_o.shape[1]),
                  whole(wo_g), whole(wo_d), whole(post_mix), whole(pre_ffn),
                  whole(wgate), whole(wup), whole(wdown), whole(post_ffn)],
        out_specs=row(d),
        out_shape=jax.ShapeDtypeStruct((m, d), f32),
        compiler_params=pltpu.CompilerParams(
            dimension_semantics=("parallel",), vmem_limit_bytes=VMEM_LIMIT_BYTES),
        name="outffn",
    )(x2, gla_o, diff_o, wo_g, wo_d, post_mix, pre_ffn, wgate, wup, wdown, post_ffn)


def kernel(x, w_in, w_gk_up, b_gk, lambda_q1, lambda_k1, lambda_q2, lambda_k2, rel_bias,
           g_gla_head, g_diff_head, w_out, pre_mix_g, post_mix_g, w_gate, w_up, w_down,
           pre_ffn_g, post_ffn_g):
    batch, seq, d = x.shape
    depth = w_in.shape[0]
    assert seq % T_ATTN == 0 and T_ATTN >= REL_MAX_DIST
    assert (batch * seq) % TM_INPROJ == 0 and (batch * seq) % TM_FFN == 0
    assert seq % (GLA_BLOCK_CHUNKS * GLA_CHUNK) == 0
    n_gla = 2 * GLA_HEADS * GLA_DK + 2 * GLA_HEADS * GLA_DV
    bias_tiles = _t5_bias_tiles(rel_bias, T_ATTN)
    x2 = x.reshape(batch * seq, d)
    for l in range(depth):
        lambda_init = 0.8 - 0.6 * math.exp(-0.3 * l)
        wg = w_in[l, :, :n_gla].astype(bf16)
        wl = jnp.pad(w_in[l, :, n_gla:n_gla + GLA_GATE_RANK],
                     ((0, 0), (0, LANES - GLA_GATE_RANK))).astype(bf16)
        wd = w_in[l, :, n_gla + GLA_GATE_RANK:].astype(bf16)
        wup = jnp.pad(w_gk_up[l], ((0, LANES - GLA_GATE_RANK), (0, 0))).astype(bf16)
        gq, gk, gv, gog, loga, dq, dk, dv = _inproj(
            x2, pre_mix_g[l][None, :], wg, wl, wd, wup, b_gk[l][None, :])
        gla_o = _gla(gq, gk, gv, gog, loga, g_gla_head[l][None, :], batch, seq)
        diff_o = _diff_attn(dq, dk, dv, bias_tiles,
                            lambda_q1[l][None, :], lambda_k1[l][None, :],
                            lambda_q2[l][None, :], lambda_k2[l][None, :],
                            g_diff_head[l][None, :], batch, seq, lambda_init)
        n_go = GLA_HEADS * GLA_DV
        x2 = _outffn(x2, gla_o, diff_o,
                     w_out[l, :n_go].astype(bf16), w_out[l, n_go:].astype(bf16),
                     post_mix_g[l][None, :], pre_ffn_g[l][None, :],
                     w_gate[l].astype(bf16), w_up[l].astype(bf16), w_down[l].astype(bf16),
                     post_ffn_g[l][None, :])
    return x2.reshape(batch, seq, d)
```

```python
import functools
import math

import jax
import jax.numpy as jnp
from jax import lax
from jax.experimental import pallas as pl
from jax.experimental.pallas import tpu as pltpu

GLA_HEADS = 4
GLA_DK = 64
GLA_DV = 128
GLA_GATE_RANK = 16
GLA_GATE_NORMALIZER = 16.0
GLA_CHUNK = 64
DIFF_HEADS = 4
DIFF_DK = 64
DIFF_DV = 128
REL_BUCKETS = 32
REL_MAX_DIST = 128
EPS = 1e-6

LANES = 128
LOG2E = 1.4426950408889634
NEG_BIG = -1e30
VMEM_LIMIT_BYTES = 56 * 1024 * 1024

TM_INPROJ = 1024
TM_FFN = 1024
T_ATTN = 256
ATTN_HEADS_PER_STEP = 2
PROJ_ROW_SPLITS = (256, 512, 768)
FFN_ROW_SPLITS = (256, 512, 768)
GLA_BLOCK_CHUNKS = 8

f32 = jnp.float32
bf16 = jnp.bfloat16


def _rms(x, g):
    return x * lax.rsqrt(jnp.mean(x * x, axis=-1, keepdims=True) + EPS) * g


def _inproj_kernel(x_ref, g_ref, wg_ref, wl_ref, wd_ref, wup_ref, bgk_ref,
                   gq_ref, gk_ref, gv_ref, gog_ref, loga_ref,
                   dq_ref, dk_ref, dv_ref):
    nq = GLA_HEADS * GLA_DK
    nv = GLA_HEADS * GLA_DV
    nd = DIFF_HEADS * 2 * DIFF_DK
    bounds = (0,) + PROJ_ROW_SPLITS + (x_ref.shape[0],)
    rows = [slice(lo, hi) for lo, hi in zip(bounds[:-1], bounds[1:])]
    h = [_rms(x_ref[rw, :], g_ref[...]).astype(bf16) for rw in rows]
    for rw, ha in zip(rows, h):
        low = jnp.dot(ha, wl_ref[...], preferred_element_type=f32)
        gla = jnp.dot(ha, wg_ref[...], preferred_element_type=f32)
        gq_ref[rw, :] = (gla[:, :nq] * (GLA_DK ** -0.5)).astype(bf16)
        gk_ref[rw, :] = gla[:, nq:2 * nq].astype(bf16)
        gv_ref[rw, :] = gla[:, 2 * nq:2 * nq + nv].astype(bf16)
        gog_ref[rw, :] = gla[:, 2 * nq + nv:].astype(bf16)
        z = jnp.dot(low.astype(bf16), wup_ref[...], preferred_element_type=f32) + bgk_ref[...]
        log_sig = jnp.minimum(z, 0.0) - jnp.log(1.0 + jnp.exp(-jnp.abs(z)))
        loga_ref[rw, :] = log_sig * (LOG2E / GLA_GATE_NORMALIZER)
    for rw, ha in zip(rows, h):
        d = jnp.dot(ha, wd_ref[...], preferred_element_type=f32)
        dq_ref[rw, :] = (d[:, :nd] * (DIFF_DK ** -0.5 * LOG2E)).astype(bf16)
        dk_ref[rw, :] = d[:, nd:2 * nd].astype(bf16)
        dv_ref[rw, :] = d[:, 2 * nd:].astype(bf16)


def _inproj(x2, g, wg, wl, wd, wup, bgk):
    m, d = x2.shape
    tm = TM_INPROJ
    nq = GLA_HEADS * GLA_DK
    nv = GLA_HEADS * GLA_DV
    nd = DIFF_HEADS * 2 * DIFF_DK
    ndv = DIFF_HEADS * DIFF_DV

    def row(n):
        return pl.BlockSpec((tm, n), lambda i: (i, 0))

    def whole(a):
        return pl.BlockSpec(a.shape, lambda i: (0, 0), pipeline_mode=pl.Buffered(1))

    out_shapes = [
        jax.ShapeDtypeStruct((m, nq), bf16), jax.ShapeDtypeStruct((m, nq), bf16),
        jax.ShapeDtypeStruct((m, nv), bf16), jax.ShapeDtypeStruct((m, nv), bf16),
        jax.ShapeDtypeStruct((m, nq), f32),
        jax.ShapeDtypeStruct((m, nd), bf16), jax.ShapeDtypeStruct((m, nd), bf16),
        jax.ShapeDtypeStruct((m, ndv), bf16),
    ]
    return pl.pallas_call(
        _inproj_kernel,
        grid=(m // tm,),
        in_specs=[row(d), whole(g), whole(wg), whole(wl), whole(wd), whole(wup), whole(bgk)],
        out_specs=[row(nq), row(nq), row(nv), row(nv), row(nq), row(nd), row(nd), row(ndv)],
        out_shape=out_shapes,
        compiler_params=pltpu.CompilerParams(
            dimension_semantics=("parallel",), vmem_limit_bytes=VMEM_LIMIT_BYTES),
        name="inproj",
    )(x2, g, wg, wl, wd, wup, bgk)


def _gla_kernel(q_ref, k_ref, v_ref, og_ref, la_ref, g_ref, o_ref, *, seq):
    c = GLA_CHUNK
    row_i = lax.broadcasted_iota(jnp.int32, (c, c), 0)
    col_i = lax.broadcasted_iota(jnp.int32, (c, c), 1)
    tril = row_i >= col_i
    tril_ones = jnp.where(tril, 1.0, 0.0).astype(bf16)
    lane = lax.broadcasted_iota(jnp.int32, (c, LANES), 1)
    head0_lanes = lane < GLA_DK
    srow = lax.broadcasted_iota(jnp.int32, (2 * GLA_DV, LANES), 0)
    slane = lax.broadcasted_iota(jnp.int32, (2 * GLA_DV, LANES), 1)
    state_mask = (srow < GLA_DV) == (slane < GLA_DK)
    g = g_ref[...]

    nt = (((1,), (1,)), ((), ()))
    tn = (((0,), (0,)), ((), ()))
    nb = GLA_BLOCK_CHUNKS
    n_blocks = seq // (nb * c)

    def stage_a(blk):
        rows = [slice((blk * nb + n) * c, (blk * nb + n + 1) * c) for n in range(nb)]
        cum = []
        for n in range(nb):
            la = la_ref[rows[n], :]
            la_hi = la.astype(bf16)
            la_lo = (la - la_hi.astype(f32)).astype(bf16)
            cum.append(jnp.dot(tril_ones, la_hi, preferred_element_type=f32)
                       + jnp.dot(tril_ones, la_lo, preferred_element_type=f32))
        return rows, cum

    def stage_b(rows, cum):
        q_heads, k_intra, k_state, q_inter, decay = [], [], [], [], []
        for n in range(nb):
            b = cum[n]
            b_last = b[c - 1:c, :]
            b_mid = b[c // 2 - 1:c // 2, :]
            q = q_ref[rows[n], :].astype(f32)
            k = k_ref[rows[n], :].astype(f32)
            qi = (q * jnp.exp2(b - b_mid)).astype(bf16)
            zero = jnp.zeros_like(qi)
            q_heads.append(jnp.concatenate([jnp.where(head0_lanes, qi, zero),
                                            jnp.where(head0_lanes, zero, qi)], axis=0))
            k_intra.append((k * jnp.exp2(b_mid - b)).astype(bf16))
            k_state.append((k * jnp.exp2(b_last - b)).astype(bf16))
            q_inter.append((q * jnp.exp2(b)).astype(bf16))
            decay.append(jnp.exp2(b_last))
        scores = [lax.dot_general(q_heads[n], k_intra[n], nt, preferred_element_type=f32)
                  for n in range(nb)]
        u_t = [lax.dot_general(v_ref[rows[n], :], k_state[n], tn, preferred_element_type=f32)
               for n in range(nb)]
        return scores, u_t, q_inter, decay

    def stage_c(rows, scores, u_t, q_inter, decay, state_t):
        states = []
        for n in range(nb):
            states.append(state_t.astype(bf16))
            state_t = state_t * decay[n] + jnp.where(state_mask, u_t[n], 0.0)
        o_inter = [lax.dot_general(q_inter[n], states[n], nt, preferred_element_type=f32)
                   for n in range(nb)]
        for n in range(nb):
            outs = []
            for hh in range(2):
                cols = slice(hh * GLA_DV, (hh + 1) * GLA_DV)
                s_h = jnp.where(tril, scores[n][hh * c:(hh + 1) * c, :], 0.0).astype(bf16)
                o_h = o_inter[n][:, cols] + jnp.dot(s_h, v_ref[rows[n], cols],
                                                    preferred_element_type=f32)
                og_h = og_ref[rows[n], cols].astype(f32)
                silu = og_h / (1.0 + jnp.exp(-og_h))
                outs.append((_rms(o_h, g) * silu).astype(bf16))
            o_ref[rows[n], :] = jnp.concatenate(outs, axis=1)
        return state_t

    state_t = jnp.zeros((2 * GLA_DV, LANES), f32)
    rows, cum = stage_a(0)
    for blk in range(n_blocks):
        mid = stage_b(rows, cum)
        cur_rows = rows
        if blk + 1 < n_blocks:
            rows, cum = stage_a(blk + 1)
        state_t = stage_c(cur_rows, *mid, state_t)


def _gla(gq, gk, gv, gog, loga, g_head, batch, seq):
    m = gq.shape[0]
    pairs = GLA_HEADS // 2
    kern = functools.partial(_gla_kernel, seq=seq)
    return pl.pallas_call(
        kern,
        grid=(batch, pairs),
        in_specs=[
            pl.BlockSpec((seq, LANES), lambda b, p: (b, p)),
            pl.BlockSpec((seq, LANES), lambda b, p: (b, p)),
            pl.BlockSpec((seq, 2 * GLA_DV), lambda b, p: (b, p)),
            pl.BlockSpec((seq, 2 * GLA_DV), lambda b, p: (b, p)),
            pl.BlockSpec((seq, LANES), lambda b, p: (b, p)),
            pl.BlockSpec((1, GLA_DV), lambda b, p: (0, 0)),
        ],
        out_specs=pl.BlockSpec((seq, 2 * GLA_DV), lambda b, p: (b, p)),
        out_shape=jax.ShapeDtypeStruct((m, GLA_HEADS * GLA_DV), bf16),
        compiler_params=pltpu.CompilerParams(
            dimension_semantics=("parallel", "parallel"), vmem_limit_bytes=VMEM_LIMIT_BYTES),
        name="gla",
    )(gq, gk, gv, gog, loga, g_head)


def _diff_kernel(q_ref, k_ref, v_ref, bias_ref, lq1_ref, lk1_ref, lq2_ref, lk2_ref, g_ref,
                 o_ref, vext_ref, *, lambda_init, seq):
    t = T_ATTN
    nq = seq // t
    heads = ATTN_HEADS_PER_STEP
    for hd in range(heads):
        vext_ref[hd, :, :DIFF_DV] = v_ref[:, hd * DIFF_DV:(hd + 1) * DIFF_DV]
        vext_ref[hd, :, DIFF_DV:] = jnp.ones((seq, DIFF_DV), bf16)
    lane = lax.broadcasted_iota(jnp.int32, (t, LANES), 1)
    map1_lanes = lane < DIFF_DK
    lam = (jnp.exp(jnp.sum(lq1_ref[...] * lk1_ref[...], axis=-1, keepdims=True))
           - jnp.exp(jnp.sum(lq2_ref[...] * lk2_ref[...], axis=-1, keepdims=True))
           + lambda_init)
    g = g_ref[...]
    nt = (((1,), (1,)), ((), ()))

    def key_ranges(i):
        pieces = []
        if i >= 2:
            pieces.append((0, (i - 1) * t, None))
        if i >= 1:
            pieces.append(((i - 1) * t, i * t, 1))
        pieces.append((i * t, (i + 1) * t, 0))
        return pieces

    def qk(hd, i):
        cols = slice(hd * LANES, (hd + 1) * LANES)
        q = q_ref[i * t:(i + 1) * t, cols]
        zero = jnp.zeros_like(q)
        qs = jnp.concatenate([jnp.where(map1_lanes, q, zero),
                              jnp.where(map1_lanes, zero, q)], axis=0)
        logits = []
        for lo, hi, bias in key_ranges(i):
            s = lax.dot_general(qs, k_ref[lo:hi, cols], nt, preferred_element_type=f32)
            if bias is not None:
                tile = bias_ref[hd, bias]
                s = s + jnp.concatenate([tile, tile], axis=0)
            logits.append(s)
        return logits

    units = [(hd, i) for hd in range(heads) for i in range(nq)]
    logits = qk(*units[0])
    for n, (hd, i) in enumerate(units):
        nxt = qk(*units[n + 1]) if n + 1 < len(units) else None
        m = functools.reduce(jnp.maximum, [jnp.max(s, axis=-1, keepdims=True) for s in logits])
        pv = None
        for (lo, hi, _), s in zip(key_ranges(i), logits):
            p = jnp.exp2(s - m).astype(bf16)
            part = jnp.dot(p, vext_ref[hd, lo:hi, :], preferred_element_type=f32)
            pv = part if pv is None else pv + part
        o = pv[:, :DIFF_DV] / pv[:, DIFF_DV:]
        o = o[:t, :] - lam * o[t:, :]
        o_ref[i * t:(i + 1) * t, hd * DIFF_DV:(hd + 1) * DIFF_DV] = (
            _rms(o, g) * (1.0 - lambda_init)).astype(bf16)
        logits = nxt


def _diff_attn(dq, dk, dv, bias_tiles, lq1, lk1, lq2, lk2, g_head, batch, seq, lambda_init):
    m = dq.shape[0]
    t = T_ATTN
    heads = ATTN_HEADS_PER_STEP
    kern = functools.partial(_diff_kernel, lambda_init=lambda_init, seq=seq)
    vec = pl.BlockSpec((1, DIFF_DK), lambda b, h: (0, 0))
    head_rows = pl.BlockSpec((seq, heads * LANES), lambda b, h: (b, h))
    return pl.pallas_call(
        kern,
        grid=(batch, DIFF_HEADS // heads),
        in_specs=[
            head_rows, head_rows, head_rows,
            pl.BlockSpec((heads, 2, t, t), lambda b, h: (h, 0, 0, 0)),
            vec, vec, vec, vec,
            pl.BlockSpec((1, DIFF_DV), lambda b, h: (0, 0)),
        ],
        out_specs=head_rows,
        out_shape=jax.ShapeDtypeStruct((m, DIFF_HEADS * DIFF_DV), bf16),
        scratch_shapes=[pltpu.VMEM((heads, seq, 2 * DIFF_DV), bf16)],
        compiler_params=pltpu.CompilerParams(
            dimension_semantics=("parallel", "parallel"),
            vmem_limit_bytes=VMEM_LIMIT_BYTES),
        name="diff_attn",
    )(dq, dk, dv, bias_tiles, lq1, lk1, lq2, lk2, g_head)


def _toeplitz(g, t):
    lead = g.shape[:-1]
    w = jnp.concatenate([g[..., 1:t + 1][..., ::-1], g[..., :1], g[..., t + 1:][..., ::-1]],
                        axis=-1)
    skew = jnp.broadcast_to(w[..., None, :], lead + (t, 2 * t)).reshape(lead + (-1,))
    return skew[..., :t * (2 * t - 1)].reshape(lead + (t, 2 * t - 1))[..., :t]


def _t5_bias_tiles(rel_bias, t):
    rel = jnp.arange(2 * t)
    max_exact = REL_BUCKETS // 2
    nf = jnp.maximum(rel, 1).astype(f32)
    large = max_exact + (jnp.log(nf / max_exact) / math.log(REL_MAX_DIST / max_exact)
                         * (REL_BUCKETS - max_exact)).astype(jnp.int32)
    large = jnp.minimum(large, REL_BUCKETS - 1)
    bucket = jnp.where(rel < max_exact, rel, large)
    table = rel_bias.astype(f32)
    by_dist = ((table[bucket] - table[REL_BUCKETS - 1][None, :]) * LOG2E).T
    masked = jnp.full((DIFF_HEADS, t), NEG_BIG, f32)
    diag = jnp.concatenate([masked, by_dist[:, :t]], axis=-1)
    return _toeplitz(jnp.stack([diag, by_dist], axis=1), t)


def _outffn_kernel(x_ref, gla_ref, diff_ref, wo_g_ref, wo_d_ref, post_mix_ref, pre_ffn_ref,
                   wgate_ref, wup_ref, wdown_ref, post_ffn_ref, o_ref):
    bounds = (0,) + FFN_ROW_SPLITS + (x_ref.shape[0],)
    rows = [slice(lo, hi) for lo, hi in zip(bounds[:-1], bounds[1:])]
    pairs = [rows[a:a + 2] for a in range(0, len(rows), 2)]

    def out_proj(pair):
        return [jnp.dot(gla_ref[rw, :], wo_g_ref[...], preferred_element_type=f32)
                + jnp.dot(diff_ref[rw, :], wo_d_ref[...], preferred_element_type=f32)
                for rw in pair]

    mix = out_proj(pairs[0])
    for n, pair in enumerate(pairs):
        x1 = [x_ref[rw, :] + _rms(mx, post_mix_ref[...]) for rw, mx in zip(pair, mix)]
        h = [_rms(xa, pre_ffn_ref[...]).astype(bf16) for xa in x1]
        gate_up = [(jnp.dot(ha, wgate_ref[...], preferred_element_type=f32),
                    jnp.dot(ha, wup_ref[...], preferred_element_type=f32)) for ha in h]
        if n + 1 < len(pairs):
            mix = out_proj(pairs[n + 1])
        f = [(gate / (1.0 + jnp.exp(-gate)) * up).astype(bf16) for gate, up in gate_up]
        y = [jnp.dot(fa, wdown_ref[...], preferred_element_type=f32) for fa in f]
        for rw, xa, ya in zip(pair, x1, y):
            o_ref[rw, :] = xa + _rms(ya, post_ffn_ref[...])


def _outffn(x2, gla_o, diff_o, wo_g, wo_d, post_mix, pre_ffn, wgate, wup, wdown, post_ffn):
    m, d = x2.shape
    tm = TM_FFN

    def row(n):
        return pl.BlockSpec((tm, n), lambda i: (i, 0))

    def whole(a):
        return pl.BlockSpec(a.shape, lambda i: (0, 0), pipeline_mode=pl.Buffered(1))

    return pl.pallas_call(
        _outffn_kernel,
        grid=(m // tm,),
        in_specs=[row(d), row(gla_o.shape[1]), row(diff_o.shape[1]),
                  whole(wo_g), whole(wo_d), whole(post_mix), whole(pre_ffn),
                  whole(wgate), whole(wup), whole(wdown), whole(post_ffn)],
        out_specs=row(d),
        out_shape=jax.ShapeDtypeStruct((m, d), f32),
        compiler_params=pltpu.CompilerParams(
            dimension_semantics=("parallel",), vmem_limit_bytes=VMEM_LIMIT_BYTES),
        name="outffn",
    )(x2, gla_o, diff_o, wo_g, wo_d, post_mix, pre_ffn, wgate, wup, wdown, post_ffn)


def kernel(x, w_in, w_gk_up, b_gk, lambda_q1, lambda_k1, lambda_q2, lambda_k2, rel_bias,
           g_gla_head, g_diff_head, w_out, pre_mix_g, post_mix_g, w_gate, w_up, w_down,
           pre_ffn_g, post_ffn_g):
    batch, seq, d = x.shape
    depth = w_in.shape[0]
    assert seq % T_ATTN == 0 and T_ATTN >= REL_MAX_DIST
    assert (batch * seq) % TM_INPROJ == 0 and (batch * seq) % TM_FFN == 0
    assert seq % (GLA_BLOCK_CHUNKS * GLA_CHUNK) == 0
    n_gla = 2 * GLA_HEADS * GLA_DK + 2 * GLA_HEADS * GLA_DV
    bias_tiles = _t5_bias_tiles(rel_bias, T_ATTN)
    x2 = x.reshape(batch * seq, d)
    for l in range(depth):
        lambda_init = 0.8 - 0.6 * math.exp(-0.3 * l)
        wg = w_in[l, :, :n_gla].astype(bf16)
        wl = jnp.pad(w_in[l, :, n_gla:n_gla + GLA_GATE_RANK],
                     ((0, 0), (0, LANES - GLA_GATE_RANK))).astype(bf16)
        wd = w_in[l, :, n_gla + GLA_GATE_RANK:].astype(bf16)
        wup = jnp.pad(w_gk_up[l], ((0, LANES - GLA_GATE_RANK), (0, 0))).astype(bf16)
        gq, gk, gv, gog, loga, dq, dk, dv = _inproj(
            x2, pre_mix_g[l][None, :], wg, wl, wd, wup, b_gk[l][None, :])
        gla_o = _gla(gq, gk, gv, gog, loga, g_gla_head[l][None, :], batch, seq)
        diff_o = _diff_attn(dq, dk, dv, bias_tiles,
                            lambda_q1[l][None, :], lambda_k1[l][None, :],
                            lambda_q2[l][None, :], lambda_k2[l][None, :],
                            g_diff_head[l][None, :], batch, seq, lambda_init)
        n_go = GLA_HEADS * GLA_DV
        x2 = _outffn(x2, gla_o, diff_o,
                     w_out[l, :n_go].astype(bf16), w_out[l, n_go:].astype(bf16),
                     post_mix_g[l][None, :], pre_ffn_g[l][None, :],
                     w_gate[l].astype(bf16), w_up[l].astype(bf16), w_down[l].astype(bf16),
                     post_ffn_g[l][None, :])
    return x2.reshape(batch, seq, d)
```

```python
import functools
import math

import jax
import jax.numpy as jnp
from jax import lax
from jax.experimental import pallas as pl
from jax.experimental.pallas import tpu as pltpu

GLA_HEADS = 4
GLA_DK = 64
GLA_DV = 128
GLA_GATE_RANK = 16
GLA_GATE_NORMALIZER = 16.0
GLA_CHUNK = 64
DIFF_HEADS = 4
DIFF_DK = 64
DIFF_DV = 128
REL_BUCKETS = 32
REL_MAX_DIST = 128
EPS = 1e-6

LANES = 128
LOG2E = 1.4426950408889634
NEG_BIG = -1e30
VMEM_LIMIT_BYTES = 56 * 1024 * 1024

TM_INPROJ = 1024
TM_FFN = 1024
T_ATTN = 256
ATTN_HEADS_PER_STEP = 2
PROJ_ROW_SPLITS = (256, 512, 768)
FFN_ROW_SPLITS = (256, 512, 768)
GLA_BLOCK_CHUNKS = 8

f32 = jnp.float32
bf16 = jnp.bfloat16


def _rms(x, g):
    return x * lax.rsqrt(jnp.mean(x * x, axis=-1, keepdims=True) + EPS) * g


def _inproj_kernel(x_ref, g_ref, w_ref, wup_ref, bgk_ref,
                   gq_ref, gk_ref, gv_ref, gog_ref, loga_ref,
                   dq_ref, dk_ref, dv_ref):
    nq = GLA_HEADS * GLA_DK
    nv = GLA_HEADS * GLA_DV
    nd = DIFF_HEADS * 2 * DIFF_DK
    n_wide = 2 * nq + 2 * nv
    bounds = (0,) + PROJ_ROW_SPLITS + (x_ref.shape[0],)
    rows = [slice(lo, hi) for lo, hi in zip(bounds[:-1], bounds[1:])]
    h = [_rms(x_ref[rw, :], g_ref[...]).astype(bf16) for rw in rows]
    for rw, ha in zip(rows, h):
        low = jnp.dot(ha, w_ref[:, 2 * n_wide:], preferred_element_type=f32)
        gla = jnp.dot(ha, w_ref[:, :n_wide], preferred_element_type=f32)
        gq_ref[rw, :] = (gla[:, :nq] * (GLA_DK ** -0.5)).astype(bf16)
        gk_ref[rw, :] = gla[:, nq:2 * nq].astype(bf16)
        gv_ref[rw, :] = gla[:, 2 * nq:2 * nq + nv].astype(bf16)
        gog_ref[rw, :] = gla[:, 2 * nq + nv:].astype(bf16)
        z = jnp.dot(low.astype(bf16), wup_ref[...], preferred_element_type=f32) + bgk_ref[...]
        log_sig = jnp.minimum(z, 0.0) - jnp.log(1.0 + jnp.exp(-jnp.abs(z)))
        loga_ref[rw, :] = log_sig * (LOG2E / GLA_GATE_NORMALIZER)
    for rw, ha in zip(rows, h):
        d = jnp.dot(ha, w_ref[:, n_wide:2 * n_wide], preferred_element_type=f32)
        dq_ref[rw, :] = (d[:, :nd] * (DIFF_DK ** -0.5 * LOG2E)).astype(bf16)
        dk_ref[rw, :] = d[:, nd:2 * nd].astype(bf16)
        dv_ref[rw, :] = d[:, 2 * nd:].astype(bf16)


def _inproj(x2, g, w, wup, bgk):
    m, d = x2.shape
    tm = TM_INPROJ
    nq = GLA_HEADS * GLA_DK
    nv = GLA_HEADS * GLA_DV
    nd = DIFF_HEADS * 2 * DIFF_DK
    ndv = DIFF_HEADS * DIFF_DV

    def row(n):
        return pl.BlockSpec((tm, n), lambda i: (i, 0))

    def whole(a):
        return pl.BlockSpec(a.shape, lambda i: (0, 0), pipeline_mode=pl.Buffered(1))

    out_shapes = [
        jax.ShapeDtypeStruct((m, nq), bf16), jax.ShapeDtypeStruct((m, nq), bf16),
        jax.ShapeDtypeStruct((m, nv), bf16), jax.ShapeDtypeStruct((m, nv), bf16),
        jax.ShapeDtypeStruct((m, nq), f32),
        jax.ShapeDtypeStruct((m, nd), bf16), jax.ShapeDtypeStruct((m, nd), bf16),
        jax.ShapeDtypeStruct((m, ndv), bf16),
    ]
    return pl.pallas_call(
        _inproj_kernel,
        grid=(m // tm,),
        in_specs=[row(d), whole(g), whole(w), whole(wup), whole(bgk)],
        out_specs=[row(nq), row(nq), row(nv), row(nv), row(nq), row(nd), row(nd), row(ndv)],
        out_shape=out_shapes,
        compiler_params=pltpu.CompilerParams(
            dimension_semantics=("parallel",), vmem_limit_bytes=VMEM_LIMIT_BYTES),
        name="inproj",
    )(x2, g, w, wup, bgk)


def _gla_kernel(q_ref, k_ref, v_ref, og_ref, la_ref, g_ref, o_ref, *, seq):
    c = GLA_CHUNK
    row_i = lax.broadcasted_iota(jnp.int32, (c, c), 0)
    col_i = lax.broadcasted_iota(jnp.int32, (c, c), 1)
    tril = row_i >= col_i
    tril_ones = jnp.where(tril, 1.0, 0.0).astype(bf16)
    lane = lax.broadcasted_iota(jnp.int32, (c, LANES), 1)
    head0_lanes = lane < GLA_DK
    srow = lax.broadcasted_iota(jnp.int32, (2 * GLA_DV, LANES), 0)
    slane = lax.broadcasted_iota(jnp.int32, (2 * GLA_DV, LANES), 1)
    state_mask = (srow < GLA_DV) == (slane < GLA_DK)
    g = g_ref[...]

    nt = (((1,), (1,)), ((), ()))
    tn = (((0,), (0,)), ((), ()))
    nb = GLA_BLOCK_CHUNKS
    n_blocks = seq // (nb * c)

    def stage_a(blk):
        rows = [slice((blk * nb + n) * c, (blk * nb + n + 1) * c) for n in range(nb)]
        cum = []
        for n in range(nb):
            la = la_ref[rows[n], :]
            la_hi = la.astype(bf16)
            la_lo = (la - la_hi.astype(f32)).astype(bf16)
            cum.append(jnp.dot(tril_ones, la_hi, preferred_element_type=f32)
                       + jnp.dot(tril_ones, la_lo, preferred_element_type=f32))
        return rows, cum

    def stage_b(rows, cum):
        q_heads, k_intra, k_state, q_inter, decay = [], [], [], [], []
        for n in range(nb):
            b = cum[n]
            b_last = b[c - 1:c, :]
            b_mid = b[c // 2 - 1:c // 2, :]
            q = q_ref[rows[n], :].astype(f32)
            k = k_ref[rows[n], :].astype(f32)
            qi = (q * jnp.exp2(b - b_mid)).astype(bf16)
            zero = jnp.zeros_like(qi)
            q_heads.append(jnp.concatenate([jnp.where(head0_lanes, qi, zero),
                                            jnp.where(head0_lanes, zero, qi)], axis=0))
            k_intra.append((k * jnp.exp2(b_mid - b)).astype(bf16))
            k_state.append((k * jnp.exp2(b_last - b)).astype(bf16))
            q_inter.append((q * jnp.exp2(b)).astype(bf16))
            decay.append(jnp.exp2(b_last))
        scores = [lax.dot_general(q_heads[n], k_intra[n], nt, preferred_element_type=f32)
                  for n in range(nb)]
        u_t = [lax.dot_general(v_ref[rows[n], :], k_state[n], tn, preferred_element_type=f32)
               for n in range(nb)]
        return scores, u_t, q_inter, decay

    def stage_c(rows, scores, u_t, q_inter, decay, state_t):
        states = []
        for n in range(nb):
            states.append(state_t.astype(bf16))
            state_t = state_t * decay[n] + jnp.where(state_mask, u_t[n], 0.0)
        o_inter = [lax.dot_general(q_inter[n], states[n], nt, preferred_element_type=f32)
                   for n in range(nb)]
        for n in range(nb):
            outs = []
            for hh in range(2):
                cols = slice(hh * GLA_DV, (hh + 1) * GLA_DV)
                s_h = jnp.where(tril, scores[n][hh * c:(hh + 1) * c, :], 0.0).astype(bf16)
                o_h = o_inter[n][:, cols] + jnp.dot(s_h, v_ref[rows[n], cols],
                                                    preferred_element_type=f32)
                og_h = og_ref[rows[n], cols].astype(f32)
                silu = og_h / (1.0 + jnp.exp(-og_h))
                outs.append((_rms(o_h, g) * silu).astype(bf16))
            o_ref[rows[n], :] = jnp.concatenate(outs, axis=1)
        return state_t

    state_t = jnp.zeros((2 * GLA_DV, LANES), f32)
    rows, cum = stage_a(0)
    for blk in range(n_blocks):
        mid = stage_b(rows, cum)
        cur_rows = rows
        if blk + 1 < n_blocks:
            rows, cum = stage_a(blk + 1)
        state_t = stage_c(cur_rows, *mid, state_t)


def _gla(gq, gk, gv, gog, loga, g_head, batch, seq):
    m = gq.shape[0]
    pairs = GLA_HEADS // 2
    kern = functools.partial(_gla_kernel, seq=seq)
    return pl.pallas_call(
        kern,
        grid=(batch, pairs),
        in_specs=[
            pl.BlockSpec((seq, LANES), lambda b, p: (b, p)),
            pl.BlockSpec((seq, LANES), lambda b, p: (b, p)),
            pl.BlockSpec((seq, 2 * GLA_DV), lambda b, p: (b, p)),
            pl.BlockSpec((seq, 2 * GLA_DV), lambda b, p: (b, p)),
            pl.BlockSpec((seq, LANES), lambda b, p: (b, p)),
            pl.BlockSpec((1, GLA_DV), lambda b, p: (0, 0)),
        ],
        out_specs=pl.BlockSpec((seq, 2 * GLA_DV), lambda b, p: (b, p)),
        out_shape=jax.ShapeDtypeStruct((m, GLA_HEADS * GLA_DV), bf16),
        compiler_params=pltpu.CompilerParams(
            dimension_semantics=("parallel", "parallel"), vmem_limit_bytes=VMEM_LIMIT_BYTES),
        name="gla",
    )(gq, gk, gv, gog, loga, g_head)


def _diff_kernel(q_ref, k_ref, v_ref, bias_ref, lq1_ref, lk1_ref, lq2_ref, lk2_ref, g_ref,
                 o_ref, vext_ref, *, lambda_init, seq):
    t = T_ATTN
    nq = seq // t
    heads = ATTN_HEADS_PER_STEP
    for hd in range(heads):
        vext_ref[hd, :, :DIFF_DV] = v_ref[:, hd * DIFF_DV:(hd + 1) * DIFF_DV]
        vext_ref[hd, :, DIFF_DV:] = jnp.ones((seq, DIFF_DV), bf16)
    lane = lax.broadcasted_iota(jnp.int32, (t, LANES), 1)
    map1_lanes = lane < DIFF_DK
    lam = (jnp.exp(jnp.sum(lq1_ref[...] * lk1_ref[...], axis=-1, keepdims=True))
           - jnp.exp(jnp.sum(lq2_ref[...] * lk2_ref[...], axis=-1, keepdims=True))
           + lambda_init)
    g = g_ref[...]
    nt = (((1,), (1,)), ((), ()))

    def key_ranges(i):
        pieces = []
        if i >= 2:
            pieces.append((0, (i - 1) * t, None))
        if i >= 1:
            pieces.append(((i - 1) * t, i * t, 1))
        pieces.append((i * t, (i + 1) * t, 0))
        return pieces

    def qk(hd, i):
        cols = slice(hd * LANES, (hd + 1) * LANES)
        q = q_ref[i * t:(i + 1) * t, cols]
        zero = jnp.zeros_like(q)
        qs = jnp.concatenate([jnp.where(map1_lanes, q, zero),
                              jnp.where(map1_lanes, zero, q)], axis=0)
        logits = []
        for lo, hi, bias in key_ranges(i):
            s = lax.dot_general(qs, k_ref[lo:hi, cols], nt, preferred_element_type=f32)
            if bias is not None:
                tile = bias_ref[hd, bias]
                s = s + jnp.concatenate([tile, tile], axis=0)
            logits.append(s)
        return logits

    units = [(hd, i) for hd in range(heads) for i in range(nq)]
    logits = qk(*units[0])
    for n, (hd, i) in enumerate(units):
        nxt = qk(*units[n + 1]) if n + 1 < len(units) else None
        m = functools.reduce(jnp.maximum, [jnp.max(s, axis=-1, keepdims=True) for s in logits])
        pv = None
        for (lo, hi, _), s in zip(key_ranges(i), logits):
            p = jnp.exp2(s - m).astype(bf16)
            part = jnp.dot(p, vext_ref[hd, lo:hi, :], preferred_element_type=f32)
            pv = part if pv is None else pv + part
        o = pv[:, :DIFF_DV] / pv[:, DIFF_DV:]
        o = o[:t, :] - lam * o[t:, :]
        o_ref[i * t:(i + 1) * t, hd * DIFF_DV:(hd + 1) * DIFF_DV] = (
            _rms(o, g) * (1.0 - lambda_init)).astype(bf16)
        logits = nxt


def _diff_attn(dq, dk, dv, bias_tiles, lq1, lk1, lq2, lk2, g_head, batch, seq, lambda_init):
    m = dq.shape[0]
    t = T_ATTN
    heads = ATTN_HEADS_PER_STEP
    kern = functools.partial(_diff_kernel, lambda_init=lambda_init, seq=seq)
    vec = pl.BlockSpec((1, DIFF_DK), lambda b, h: (0, 0))
    head_rows = pl.BlockSpec((seq, heads * LANES), lambda b, h: (b, h))
    return pl.pallas_call(
        kern,
        grid=(batch, DIFF_HEADS // heads),
        in_specs=[
            head_rows, head_rows, head_rows,
            pl.BlockSpec((heads, 2, t, t), lambda b, h: (h, 0, 0, 0)),
            vec, vec, vec, vec,
            pl.BlockSpec((1, DIFF_DV), lambda b, h: (0, 0)),
        ],
        out_specs=head_rows,
        out_shape=jax.ShapeDtypeStruct((m, DIFF_HEADS * DIFF_DV), bf16),
        scratch_shapes=[pltpu.VMEM((heads, seq, 2 * DIFF_DV), bf16)],
        compiler_params=pltpu.CompilerParams(
            dimension_semantics=("parallel", "parallel"),
            vmem_limit_bytes=VMEM_LIMIT_BYTES),
        name="diff_attn",
    )(dq, dk, dv, bias_tiles, lq1, lk1, lq2, lk2, g_head)


def _toeplitz_kernel(w_ref, o_ref):
    t = o_ref.shape[-1]
    for kind in range(o_ref.shape[1]):
        rows = jnp.broadcast_to(w_ref[0, kind], (t, 2 * t))
        o_ref[0, kind] = pltpu.roll(rows, 0, 1, stride=1, stride_axis=0)[:, :t]


def _toeplitz(g, t):
    heads, kinds = g.shape[:2]
    w = jnp.concatenate([g[..., 1:t + 1][..., ::-1], g[..., :1], g[..., t + 1:][..., ::-1]],
                        axis=-1)
    return pl.pallas_call(
        _toeplitz_kernel,
        grid=(heads,),
        in_specs=[pl.BlockSpec((1, kinds, 1, 2 * t), lambda h: (h, 0, 0, 0))],
        out_specs=pl.BlockSpec((1, kinds, t, t), lambda h: (h, 0, 0, 0)),
        out_shape=jax.ShapeDtypeStruct((heads, kinds, t, t), f32),
        name="bias_tiles",
    )(w[:, :, None, :])


def _t5_bias_tiles(rel_bias, t):
    rel = jnp.arange(2 * t)
    max_exact = REL_BUCKETS // 2
    nf = jnp.maximum(rel, 1).astype(f32)
    large = max_exact + (jnp.log(nf / max_exact) / math.log(REL_MAX_DIST / max_exact)
                         * (REL_BUCKETS - max_exact)).astype(jnp.int32)
    large = jnp.minimum(large, REL_BUCKETS - 1)
    bucket = jnp.where(rel < max_exact, rel, large)
    table = rel_bias.astype(f32)
    by_dist = ((table[bucket] - table[REL_BUCKETS - 1][None, :]) * LOG2E).T
    masked = jnp.full((DIFF_HEADS, t), NEG_BIG, f32)
    diag = jnp.concatenate([masked, by_dist[:, :t]], axis=-1)
    return _toeplitz(jnp.stack([diag, by_dist], axis=1), t)


def _outffn_kernel(x_ref, gla_ref, diff_ref, wo_ref, post_mix_ref, pre_ffn_ref,
                   wgate_ref, wup_ref, wdown_ref, post_ffn_ref, o_ref):
    bounds = (0,) + FFN_ROW_SPLITS + (x_ref.shape[0],)
    rows = [slice(lo, hi) for lo, hi in zip(bounds[:-1], bounds[1:])]
    pairs = [rows[a:a + 2] for a in range(0, len(rows), 2)]
    n_gla = gla_ref.shape[1]

    def out_proj(pair):
        return [jnp.dot(gla_ref[rw, :], wo_ref[:n_gla, :], preferred_element_type=f32)
                + jnp.dot(diff_ref[rw, :], wo_ref[n_gla:, :], preferred_element_type=f32)
                for rw in pair]

    mix = out_proj(pairs[0])
    for n, pair in enumerate(pairs):
        x1 = [x_ref[rw, :] + _rms(mx, post_mix_ref[...]) for rw, mx in zip(pair, mix)]
        h = [_rms(xa, pre_ffn_ref[...]).astype(bf16) for xa in x1]
        gate_up = [(jnp.dot(ha, wgate_ref[...], preferred_element_type=f32),
                    jnp.dot(ha, wup_ref[...], preferred_element_type=f32)) for ha in h]
        if n + 1 < len(pairs):
            mix = out_proj(pairs[n + 1])
        f = [(gate / (1.0 + jnp.exp(-gate)) * up).astype(bf16) for gate, up in gate_up]
        y = [jnp.dot(fa, wdown_ref[...], preferred_element_type=f32) for fa in f]
        for rw, xa, ya in zip(pair, x1, y):
            o_ref[rw, :] = xa + _rms(ya, post_ffn_ref[...])


def _outffn(x2, gla_o, diff_o, wo, post_mix, pre_ffn, wgate, wup, wdown, post_ffn):
    m, d = x2.shape
    tm = TM_FFN

    def row(n):
        return pl.BlockSpec((tm, n), lambda i: (i, 0))

    def whole(a):
        return pl.BlockSpec(a.shape, lambda i: (0, 0), pipeline_mode=pl.Buffered(1))

    return pl.pallas_call(
        _outffn_kernel,
        grid=(m // tm,),
        in_specs=[row(d), row(gla_o.shape[1]), row(diff_o.shape[1]),
                  whole(wo), whole(post_mix), whole(pre_ffn),
                  whole(wgate), whole(wup), whole(wdown), whole(post_ffn)],
        out_specs=row(d),
        out_shape=jax.ShapeDtypeStruct((m, d), f32),
        compiler_params=pltpu.CompilerParams(
            dimension_semantics=("parallel",), vmem_limit_bytes=VMEM_LIMIT_BYTES),
        name="outffn",
    )(x2, gla_o, diff_o, wo, post_mix, pre_ffn, wgate, wup, wdown, post_ffn)


def kernel(x, w_in, w_gk_up, b_gk, lambda_q1, lambda_k1, lambda_q2, lambda_k2, rel_bias,
           g_gla_head, g_diff_head, w_out, pre_mix_g, post_mix_g, w_gate, w_up, w_down,
           pre_ffn_g, post_ffn_g):
    batch, seq, d = x.shape
    depth = w_in.shape[0]
    assert seq % T_ATTN == 0 and T_ATTN >= REL_MAX_DIST
    assert (batch * seq) % TM_INPROJ == 0 and (batch * seq) % TM_FFN == 0
    assert seq % (GLA_BLOCK_CHUNKS * GLA_CHUNK) == 0
    n_gla = 2 * GLA_HEADS * GLA_DK + 2 * GLA_HEADS * GLA_DV
    bias_tiles = _t5_bias_tiles(rel_bias, T_ATTN)
    x2 = x.reshape(batch * seq, d)
    for l in range(depth):
        lambda_init = 0.8 - 0.6 * math.exp(-0.3 * l)
        w = jnp.concatenate(
            [w_in[l, :, :n_gla], w_in[l, :, n_gla + GLA_GATE_RANK:],
             w_in[l, :, n_gla:n_gla + GLA_GATE_RANK],
             jnp.zeros((d, LANES - GLA_GATE_RANK), w_in.dtype)], axis=1).astype(bf16)
        wup = jnp.pad(w_gk_up[l], ((0, LANES - GLA_GATE_RANK), (0, 0))).astype(bf16)
        gq, gk, gv, gog, loga, dq, dk, dv = _inproj(
            x2, pre_mix_g[l][None, :], w, wup, b_gk[l][None, :])
        gla_o = _gla(gq, gk, gv, gog, loga, g_gla_head[l][None, :], batch, seq)
        diff_o = _diff_attn(dq, dk, dv, bias_tiles,
                            lambda_q1[l][None, :], lambda_k1[l][None, :],
                            lambda_q2[l][None, :], lambda_k2[l][None, :],
                            g_diff_head[l][None, :], batch, seq, lambda_init)
        x2 = _outffn(x2, gla_o, diff_o, w_out[l].astype(bf16),
                     post_mix_g[l][None, :], pre_ffn_g[l][None, :],
                     w_gate[l].astype(bf16), w_up[l].astype(bf16), w_down[l].astype(bf16),
                     post_ffn_g[l][None, :])
    return x2.reshape(batch, seq, d)
```

```python
import functools
import math

import jax
import jax.numpy as jnp
from jax import lax
from jax.experimental import pallas as pl
from jax.experimental.pallas import tpu as pltpu

GLA_HEADS = 4
GLA_DK = 64
GLA_DV = 128
GLA_GATE_RANK = 16
GLA_GATE_NORMALIZER = 16.0
GLA_CHUNK = 64
DIFF_HEADS = 4
DIFF_DK = 64
DIFF_DV = 128
REL_BUCKETS = 32
REL_MAX_DIST = 128
EPS = 1e-6

LANES = 128
LOG2E = 1.4426950408889634
NEG_BIG = -1e30
VMEM_LIMIT_BYTES = 56 * 1024 * 1024

TM_INPROJ = 1024
TM_FFN = 1024
T_ATTN = 256
ATTN_HEADS_PER_STEP = 2
PROJ_ROW_SPLITS = (256, 512, 768)
FFN_ROW_SPLITS = (256, 512, 768)
GLA_BLOCK_CHUNKS = 8

f32 = jnp.float32
bf16 = jnp.bfloat16


def _rms(x, g):
    return x * lax.rsqrt(jnp.mean(x * x, axis=-1, keepdims=True) + EPS) * g


def _inproj_kernel(x_ref, g_ref, wg_ref, wl_ref, wd_ref, wup_ref, bgk_ref,
                   gq_ref, gk_ref, gv_ref, gog_ref, loga_ref,
                   dq_ref, dk_ref, dv_ref):
    nq = GLA_HEADS * GLA_DK
    nv = GLA_HEADS * GLA_DV
    nd = DIFF_HEADS * 2 * DIFF_DK
    bounds = (0,) + PROJ_ROW_SPLITS + (x_ref.shape[0],)
    rows = [slice(lo, hi) for lo, hi in zip(bounds[:-1], bounds[1:])]
    h = [_rms(x_ref[rw, :], g_ref[...]).astype(bf16) for rw in rows]
    for rw, ha in zip(rows, h):
        low = jnp.dot(ha, wl_ref[...], preferred_element_type=f32)
        gla = jnp.dot(ha, wg_ref[...], preferred_element_type=f32)
        gq_ref[rw, :] = (gla[:, :nq] * (GLA_DK ** -0.5)).astype(bf16)
        gk_ref[rw, :] = gla[:, nq:2 * nq].astype(bf16)
        gv_ref[rw, :] = gla[:, 2 * nq:2 * nq + nv].astype(bf16)
        gog_ref[rw, :] = gla[:, 2 * nq + nv:].astype(bf16)
        z = jnp.dot(low.astype(bf16), wup_ref[...], preferred_element_type=f32) + bgk_ref[...]
        log_sig = jnp.minimum(z, 0.0) - jnp.log(1.0 + jnp.exp(-jnp.abs(z)))
        loga_ref[rw, :] = log_sig * (LOG2E / GLA_GATE_NORMALIZER)
    for rw, ha in zip(rows, h):
        d = jnp.dot(ha, wd_ref[...], preferred_element_type=f32)
        dq_ref[rw, :] = (d[:, :nd] * (DIFF_DK ** -0.5 * LOG2E)).astype(bf16)
        dk_ref[rw, :] = d[:, nd:2 * nd].astype(bf16)
        dv_ref[rw, :] = d[:, 2 * nd:].astype(bf16)


def _inproj(x2, g, wg, wl, wd, wup, bgk):
    m, d = x2.shape
    tm = TM_INPROJ
    nq = GLA_HEADS * GLA_DK
    nv = GLA_HEADS * GLA_DV
    nd = DIFF_HEADS * 2 * DIFF_DK
    ndv = DIFF_HEADS * DIFF_DV

    def row(n):
        return pl.BlockSpec((tm, n), lambda i: (i, 0))

    def whole(a):
        return pl.BlockSpec(a.shape, lambda i: (0, 0), pipeline_mode=pl.Buffered(1))

    out_shapes = [
        jax.ShapeDtypeStruct((m, nq), bf16), jax.ShapeDtypeStruct((m, nq), bf16),
        jax.ShapeDtypeStruct((m, nv), bf16), jax.ShapeDtypeStruct((m, nv), bf16),
        jax.ShapeDtypeStruct((m, nq), f32),
        jax.ShapeDtypeStruct((m, nd), bf16), jax.ShapeDtypeStruct((m, nd), bf16),
        jax.ShapeDtypeStruct((m, ndv), bf16),
    ]
    return pl.pallas_call(
        _inproj_kernel,
        grid=(m // tm,),
        in_specs=[row(d), whole(g), whole(wg), whole(wl), whole(wd), whole(wup), whole(bgk)],
        out_specs=[row(nq), row(nq), row(nv), row(nv), row(nq), row(nd), row(nd), row(ndv)],
        out_shape=out_shapes,
        compiler_params=pltpu.CompilerParams(
            dimension_semantics=("parallel",), vmem_limit_bytes=VMEM_LIMIT_BYTES),
        name="inproj",
    )(x2, g, wg, wl, wd, wup, bgk)


def _gla_kernel(q_ref, k_ref, v_ref, la_ref, o_ref, *, seq):
    c = GLA_CHUNK
    row_i = lax.broadcasted_iota(jnp.int32, (c, c), 0)
    col_i = lax.broadcasted_iota(jnp.int32, (c, c), 1)
    tril = row_i >= col_i
    tril_ones = jnp.where(tril, 1.0, 0.0).astype(bf16)
    lane = lax.broadcasted_iota(jnp.int32, (c, LANES), 1)
    head0_lanes = lane < GLA_DK
    srow = lax.broadcasted_iota(jnp.int32, (2 * GLA_DV, LANES), 0)
    slane = lax.broadcasted_iota(jnp.int32, (2 * GLA_DV, LANES), 1)
    state_mask = (srow < GLA_DV) == (slane < GLA_DK)

    nt = (((1,), (1,)), ((), ()))
    tn = (((0,), (0,)), ((), ()))
    nb = GLA_BLOCK_CHUNKS
    n_blocks = seq // (nb * c)

    def stage_a(blk):
        rows = [slice((blk * nb + n) * c, (blk * nb + n + 1) * c) for n in range(nb)]
        cum = []
        for n in range(nb):
            la = la_ref[rows[n], :]
            la_hi = la.astype(bf16)
            la_lo = (la - la_hi.astype(f32)).astype(bf16)
            cum.append(jnp.dot(tril_ones, la_hi, preferred_element_type=f32)
                       + jnp.dot(tril_ones, la_lo, preferred_element_type=f32))
        return rows, cum

    def stage_b(rows, cum):
        q_heads, k_intra, k_state, q_inter, decay = [], [], [], [], []
        for n in range(nb):
            b = cum[n]
            b_last = b[c - 1:c, :]
            b_mid = b[c // 2 - 1:c // 2, :]
            q = q_ref[rows[n], :].astype(f32)
            k = k_ref[rows[n], :].astype(f32)
            qi = (q * jnp.exp2(b - b_mid)).astype(bf16)
            zero = jnp.zeros_like(qi)
            q_heads.append(jnp.concatenate([jnp.where(head0_lanes, qi, zero),
                                            jnp.where(head0_lanes, zero, qi)], axis=0))
            k_intra.append((k * jnp.exp2(b_mid - b)).astype(bf16))
            k_state.append((k * jnp.exp2(b_last - b)).astype(bf16))
            q_inter.append((q * jnp.exp2(b)).astype(bf16))
            decay.append(jnp.exp2(b_last))
        scores = [lax.dot_general(q_heads[n], k_intra[n], nt, preferred_element_type=f32)
                  for n in range(nb)]
        u_t = [lax.dot_general(v_ref[rows[n], :], k_state[n], tn, preferred_element_type=f32)
               for n in range(nb)]
        return scores, u_t, q_inter, decay

    def stage_c(rows, scores, u_t, q_inter, decay, state_t):
        states = []
        for n in range(nb):
            states.append(state_t.astype(bf16))
            state_t = state_t * decay[n] + jnp.where(state_mask, u_t[n], 0.0)
        o_inter = [lax.dot_general(q_inter[n], states[n], nt, preferred_element_type=f32)
                   for n in range(nb)]
        for n in range(nb):
            outs = []
            for hh in range(2):
                cols = slice(hh * GLA_DV, (hh + 1) * GLA_DV)
                s_h = jnp.where(tril, scores[n][hh * c:(hh + 1) * c, :], 0.0).astype(bf16)
                o_h = o_inter[n][:, cols] + jnp.dot(s_h, v_ref[rows[n], cols],
                                                    preferred_element_type=f32)
                outs.append(o_h.astype(bf16))
            o_ref[rows[n], :] = jnp.concatenate(outs, axis=1)
        return state_t

    state_t = jnp.zeros((2 * GLA_DV, LANES), f32)
    rows, cum = stage_a(0)
    for blk in range(n_blocks):
        mid = stage_b(rows, cum)
        cur_rows = rows
        if blk + 1 < n_blocks:
            rows, cum = stage_a(blk + 1)
        state_t = stage_c(cur_rows, *mid, state_t)


def _gla(gq, gk, gv, loga, batch, seq):
    m = gq.shape[0]
    pairs = GLA_HEADS // 2
    kern = functools.partial(_gla_kernel, seq=seq)
    return pl.pallas_call(
        kern,
        grid=(batch, pairs),
        in_specs=[
            pl.BlockSpec((seq, LANES), lambda b, p: (b, p)),
            pl.BlockSpec((seq, LANES), lambda b, p: (b, p)),
            pl.BlockSpec((seq, 2 * GLA_DV), lambda b, p: (b, p)),
            pl.BlockSpec((seq, LANES), lambda b, p: (b, p)),
        ],
        out_specs=pl.BlockSpec((seq, 2 * GLA_DV), lambda b, p: (b, p)),
        out_shape=jax.ShapeDtypeStruct((m, GLA_HEADS * GLA_DV), bf16),
        compiler_params=pltpu.CompilerParams(
            dimension_semantics=("parallel", "parallel"), vmem_limit_bytes=VMEM_LIMIT_BYTES),
        name="gla",
    )(gq, gk, gv, loga)


def _diff_kernel(q_ref, k_ref, v_ref, bias_ref, lq1_ref, lk1_ref, lq2_ref, lk2_ref, g_ref,
                 o_ref, vext_ref, *, lambda_init, seq):
    t = T_ATTN
    nq = seq // t
    heads = ATTN_HEADS_PER_STEP
    for hd in range(heads):
        vext_ref[hd, :, :DIFF_DV] = v_ref[:, hd * DIFF_DV:(hd + 1) * DIFF_DV]
        vext_ref[hd, :, DIFF_DV:] = jnp.ones((seq, DIFF_DV), bf16)
    lane = lax.broadcasted_iota(jnp.int32, (t, LANES), 1)
    map1_lanes = lane < DIFF_DK
    lam = (jnp.exp(jnp.sum(lq1_ref[...] * lk1_ref[...], axis=-1, keepdims=True))
           - jnp.exp(jnp.sum(lq2_ref[...] * lk2_ref[...], axis=-1, keepdims=True))
           + lambda_init)
    g = g_ref[...]
    nt = (((1,), (1,)), ((), ()))

    def key_ranges(i):
        pieces = []
        if i >= 2:
            pieces.append((0, (i - 1) * t, None))
        if i >= 1:
            pieces.append(((i - 1) * t, i * t, 1))
        pieces.append((i * t, (i + 1) * t, 0))
        return pieces

    def qk(hd, i):
        cols = slice(hd * LANES, (hd + 1) * LANES)
        q = q_ref[i * t:(i + 1) * t, cols]
        zero = jnp.zeros_like(q)
        qs = jnp.concatenate([jnp.where(map1_lanes, q, zero),
                              jnp.where(map1_lanes, zero, q)], axis=0)
        logits = []
        for lo, hi, bias in key_ranges(i):
            s = lax.dot_general(qs, k_ref[lo:hi, cols], nt, preferred_element_type=f32)
            if bias is not None:
                tile = bias_ref[hd, bias]
                s = s + jnp.concatenate([tile, tile], axis=0)
            logits.append(s)
        return logits

    units = [(hd, i) for hd in range(heads) for i in range(nq)]
    logits = qk(*units[0])
    for n, (hd, i) in enumerate(units):
        nxt = qk(*units[n + 1]) if n + 1 < len(units) else None
        m = functools.reduce(jnp.maximum, [jnp.max(s, axis=-1, keepdims=True) for s in logits])
        pv = None
        for (lo, hi, _), s in zip(key_ranges(i), logits):
            p = jnp.exp2(s - m).astype(bf16)
            part = jnp.dot(p, vext_ref[hd, lo:hi, :], preferred_element_type=f32)
            pv = part if pv is None else pv + part
        o = pv[:, :DIFF_DV] / pv[:, DIFF_DV:]
        o = o[:t, :] - lam * o[t:, :]
        o_ref[i * t:(i + 1) * t, hd * DIFF_DV:(hd + 1) * DIFF_DV] = (
            _rms(o, g) * (1.0 - lambda_init)).astype(bf16)
        logits = nxt


def _diff_attn(dq, dk, dv, bias_tiles, lq1, lk1, lq2, lk2, g_head, batch, seq, lambda_init):
    m = dq.shape[0]
    t = T_ATTN
    heads = ATTN_HEADS_PER_STEP
    kern = functools.partial(_diff_kernel, lambda_init=lambda_init, seq=seq)
    vec = pl.BlockSpec((1, DIFF_DK), lambda b, h: (0, 0))
    head_rows = pl.BlockSpec((seq, heads * LANES), lambda b, h: (b, h))
    return pl.pallas_call(
        kern,
        grid=(batch, DIFF_HEADS // heads),
        in_specs=[
            head_rows, head_rows, head_rows,
            pl.BlockSpec((heads, 2, t, t), lambda b, h: (h, 0, 0, 0)),
            vec, vec, vec, vec,
            pl.BlockSpec((1, DIFF_DV), lambda b, h: (0, 0)),
        ],
        out_specs=head_rows,
        out_shape=jax.ShapeDtypeStruct((m, DIFF_HEADS * DIFF_DV), bf16),
        scratch_shapes=[pltpu.VMEM((heads, seq, 2 * DIFF_DV), bf16)],
        compiler_params=pltpu.CompilerParams(
            dimension_semantics=("parallel", "parallel"),
            vmem_limit_bytes=VMEM_LIMIT_BYTES),
        name="diff_attn",
    )(dq, dk, dv, bias_tiles, lq1, lk1, lq2, lk2, g_head)


def _toeplitz_kernel(w_ref, o_ref):
    t = o_ref.shape[-1]
    for kind in range(o_ref.shape[1]):
        rows = jnp.broadcast_to(w_ref[0, kind], (t, 2 * t))
        o_ref[0, kind] = pltpu.roll(rows, 0, 1, stride=1, stride_axis=0)[:, :t]


def _toeplitz(g, t):
    heads, kinds = g.shape[:2]
    w = jnp.concatenate([g[..., 1:t + 1][..., ::-1], g[..., :1], g[..., t + 1:][..., ::-1]],
                        axis=-1)
    return pl.pallas_call(
        _toeplitz_kernel,
        grid=(heads,),
        in_specs=[pl.BlockSpec((1, kinds, 1, 2 * t), lambda h: (h, 0, 0, 0))],
        out_specs=pl.BlockSpec((1, kinds, t, t), lambda h: (h, 0, 0, 0)),
        out_shape=jax.ShapeDtypeStruct((heads, kinds, t, t), f32),
        name="bias_tiles",
    )(w[:, :, None, :])


def _t5_bias_tiles(rel_bias, t):
    rel = jnp.arange(2 * t)
    max_exact = REL_BUCKETS // 2
    nf = jnp.maximum(rel, 1).astype(f32)
    large = max_exact + (jnp.log(nf / max_exact) / math.log(REL_MAX_DIST / max_exact)
                         * (REL_BUCKETS - max_exact)).astype(jnp.int32)
    large = jnp.minimum(large, REL_BUCKETS - 1)
    bucket = jnp.where(rel < max_exact, rel, large)
    table = rel_bias.astype(f32)
    by_dist = ((table[bucket] - table[REL_BUCKETS - 1][None, :]) * LOG2E).T
    masked = jnp.full((DIFF_HEADS, t), NEG_BIG, f32)
    diag = jnp.concatenate([masked, by_dist[:, :t]], axis=-1)
    return _toeplitz(jnp.stack([diag, by_dist], axis=1), t)


def _outffn_kernel(x_ref, gla_ref, gog_ref, diff_ref, g_gla_ref, wo_ref, post_mix_ref, pre_ffn_ref,
                   wgate_ref, wup_ref, wdown_ref, post_ffn_ref, o_ref):
    bounds = (0,) + FFN_ROW_SPLITS + (x_ref.shape[0],)
    rows = [slice(lo, hi) for lo, hi in zip(bounds[:-1], bounds[1:])]
    pairs = [rows[a:a + 2] for a in range(0, len(rows), 2)]
    n_gla = gla_ref.shape[1]

    def gla_gated(rw):
        heads = []
        for hh in range(GLA_HEADS):
            cols = slice(hh * GLA_DV, (hh + 1) * GLA_DV)
            og = gog_ref[rw, cols].astype(f32)
            silu = og / (1.0 + jnp.exp(-og))
            heads.append((_rms(gla_ref[rw, cols].astype(f32), g_gla_ref[...]) * silu).astype(bf16))
        return jnp.concatenate(heads, axis=1)

    def out_proj(pair):
        return [jnp.dot(diff_ref[rw, :], wo_ref[n_gla:, :], preferred_element_type=f32)
                + jnp.dot(gla_gated(rw), wo_ref[:n_gla, :], preferred_element_type=f32)
                for rw in pair]

    def gate_and_up(ha):
        return (jnp.dot(ha, wgate_ref[...], preferred_element_type=f32),
                jnp.dot(ha, wup_ref[...], preferred_element_type=f32))

    mix = out_proj(pairs[0])
    for n, pair in enumerate(pairs):
        x1 = [x_ref[rw, :] + _rms(mx, post_mix_ref[...]) for rw, mx in zip(pair, mix)]
        h = [_rms(xa, pre_ffn_ref[...]).astype(bf16) for xa in x1]
        gate_up = [gate_and_up(h[0])]
        if n + 1 < len(pairs):
            mix = out_proj(pairs[n + 1])
        gate_up.append(gate_and_up(h[1]))
        f = [(gate / (1.0 + jnp.exp(-gate)) * up).astype(bf16) for gate, up in gate_up]
        y = [jnp.dot(fa, wdown_ref[...], preferred_element_type=f32) for fa in f]
        for rw, xa, ya in zip(pair, x1, y):
            o_ref[rw, :] = xa + _rms(ya, post_ffn_ref[...])


def _outffn(x2, gla_o, gog, diff_o, g_gla, wo, post_mix, pre_ffn, wgate, wup, wdown, post_ffn):
    m, d = x2.shape
    tm = TM_FFN

    def row(n):
        return pl.BlockSpec((tm, n), lambda i: (i, 0))

    def whole(a):
        return pl.BlockSpec(a.shape, lambda i: (0, 0), pipeline_mode=pl.Buffered(1))

    return pl.pallas_call(
        _outffn_kernel,
        grid=(m // tm,),
        in_specs=[row(d), row(gla_o.shape[1]), row(gog.shape[1]), row(diff_o.shape[1]),
                  whole(g_gla), whole(wo), whole(post_mix), whole(pre_ffn),
                  whole(wgate), whole(wup), whole(wdown), whole(post_ffn)],
        out_specs=row(d),
        out_shape=jax.ShapeDtypeStruct((m, d), f32),
        compiler_params=pltpu.CompilerParams(
            dimension_semantics=("parallel",), vmem_limit_bytes=VMEM_LIMIT_BYTES),
        name="outffn",
    )(x2, gla_o, gog, diff_o, g_gla, wo, post_mix, pre_ffn, wgate, wup, wdown, post_ffn)


def kernel(x, w_in, w_gk_up, b_gk, lambda_q1, lambda_k1, lambda_q2, lambda_k2, rel_bias,
           g_gla_head, g_diff_head, w_out, pre_mix_g, post_mix_g, w_gate, w_up, w_down,
           pre_ffn_g, post_ffn_g):
    batch, seq, d = x.shape
    depth = w_in.shape[0]
    assert seq % T_ATTN == 0 and T_ATTN >= REL_MAX_DIST
    assert (batch * seq) % TM_INPROJ == 0 and (batch * seq) % TM_FFN == 0
    assert seq % (GLA_BLOCK_CHUNKS * GLA_CHUNK) == 0
    n_gla = 2 * GLA_HEADS * GLA_DK + 2 * GLA_HEADS * GLA_DV
    bias_tiles = _t5_bias_tiles(rel_bias, T_ATTN)
    x2 = x.reshape(batch * seq, d)
    for l in range(depth):
        lambda_init = 0.8 - 0.6 * math.exp(-0.3 * l)
        wg = w_in[l, :, :n_gla].astype(bf16)
        wl = jnp.pad(w_in[l, :, n_gla:n_gla + GLA_GATE_RANK],
                     ((0, 0), (0, LANES - GLA_GATE_RANK))).astype(bf16)
        wd = w_in[l, :, n_gla + GLA_GATE_RANK:].astype(bf16)
        wup = jnp.pad(w_gk_up[l], ((0, LANES - GLA_GATE_RANK), (0, 0))).astype(bf16)
        gq, gk, gv, gog, loga, dq, dk, dv = _inproj(
            x2, pre_mix_g[l][None, :], wg, wl, wd, wup, b_gk[l][None, :])
        gla_o = _gla(gq, gk, gv, loga, batch, seq)
        diff_o = _diff_attn(dq, dk, dv, bias_tiles,
                            lambda_q1[l][None, :], lambda_k1[l][None, :],
                            lambda_q2[l][None, :], lambda_k2[l][None, :],
                            g_diff_head[l][None, :], batch, seq, lambda_init)
        x2 = _outffn(x2, gla_o, gog, diff_o, g_gla_head[l][None, :], w_out[l].astype(bf16),
                     post_mix_g[l][None, :], pre_ffn_g[l][None, :],
                     w_gate[l].astype(bf16), w_up[l].astype(bf16), w_down[l].astype(bf16),
                     post_ffn_g[l][None, :])
    return x2.reshape(batch, seq, d)
```

```python
import functools
import math

import jax
import jax.numpy as jnp
from jax import lax
from jax.experimental import pallas as pl
from jax.experimental.pallas import tpu as pltpu

GLA_HEADS = 4
GLA_DK = 64
GLA_DV = 128
GLA_GATE_RANK = 16
GLA_GATE_NORMALIZER = 16.0
GLA_CHUNK = 64
DIFF_HEADS = 4
DIFF_DK = 64
DIFF_DV = 128
REL_BUCKETS = 32
REL_MAX_DIST = 128
EPS = 1e-6

LANES = 128
LOG2E = 1.4426950408889634
NEG_BIG = -1e30
VMEM_LIMIT_BYTES = 56 * 1024 * 1024

TM_INPROJ = 1024
TM_FFN = 1024
T_ATTN = 256
ATTN_HEADS_PER_STEP = 2
PROJ_ROW_SPLITS = (256, 512, 768)
FFN_ROW_SPLITS = (256, 512, 768)
GLA_BLOCK_CHUNKS = 8

f32 = jnp.float32
bf16 = jnp.bfloat16


def _rms(x, g):
    return x * lax.rsqrt(jnp.mean(x * x, axis=-1, keepdims=True) + EPS) * g


def _inproj_kernel(x_ref, g_ref, wg_ref, wl_ref, wd_ref, wup_ref, bgk_ref,
                   gq_ref, gk_ref, gv_ref, gog_ref, loga_ref,
                   dq_ref, dk_ref, dv_ref):
    nq = GLA_HEADS * GLA_DK
    nv = GLA_HEADS * GLA_DV
    nd = DIFF_HEADS * 2 * DIFF_DK
    bounds = (0,) + PROJ_ROW_SPLITS + (x_ref.shape[0],)
    rows = [slice(lo, hi) for lo, hi in zip(bounds[:-1], bounds[1:])]
    h = [_rms(x_ref[rw, :], g_ref[...]).astype(bf16) for rw in rows]
    for rw, ha in zip(rows, h):
        low = jnp.dot(ha, wl_ref[...], preferred_element_type=f32)
        gla = jnp.dot(ha, wg_ref[...], preferred_element_type=f32)
        gq_ref[rw, :] = (gla[:, :nq] * (GLA_DK ** -0.5)).astype(bf16)
        gk_ref[rw, :] = gla[:, nq:2 * nq].astype(bf16)
        gv_ref[rw, :] = gla[:, 2 * nq:2 * nq + nv].astype(bf16)
        gog_ref[rw, :] = gla[:, 2 * nq + nv:].astype(bf16)
        z = jnp.dot(low.astype(bf16), wup_ref[...], preferred_element_type=f32) + bgk_ref[...]
        log_sig = jnp.minimum(z, 0.0) - jnp.log(1.0 + jnp.exp(-jnp.abs(z)))
        loga_ref[rw, :] = log_sig * (LOG2E / GLA_GATE_NORMALIZER)
    for rw, ha in zip(rows, h):
        d = jnp.dot(ha, wd_ref[...], preferred_element_type=f32)
        dq_ref[rw, :] = (d[:, :nd] * (DIFF_DK ** -0.5 * LOG2E)).astype(bf16)
        dk_ref[rw, :] = d[:, nd:2 * nd].astype(bf16)
        dv_ref[rw, :] = d[:, 2 * nd:].astype(bf16)


def _inproj(x2, g, wg, wl, wd, wup, bgk):
    m, d = x2.shape
    tm = TM_INPROJ
    nq = GLA_HEADS * GLA_DK
    nv = GLA_HEADS * GLA_DV
    nd = DIFF_HEADS * 2 * DIFF_DK
    ndv = DIFF_HEADS * DIFF_DV

    def row(n):
        return pl.BlockSpec((tm, n), lambda i: (i, 0))

    def whole(a):
        return pl.BlockSpec(a.shape, lambda i: (0, 0), pipeline_mode=pl.Buffered(1))

    out_shapes = [
        jax.ShapeDtypeStruct((m, nq), bf16), jax.ShapeDtypeStruct((m, nq), bf16),
        jax.ShapeDtypeStruct((m, nv), bf16), jax.ShapeDtypeStruct((m, nv), bf16),
        jax.ShapeDtypeStruct((m, nq), f32),
        jax.ShapeDtypeStruct((m, nd), bf16), jax.ShapeDtypeStruct((m, nd), bf16),
        jax.ShapeDtypeStruct((m, ndv), bf16),
    ]
    return pl.pallas_call(
        _inproj_kernel,
        grid=(m // tm,),
        in_specs=[row(d), whole(g), whole(wg), whole(wl), whole(wd), whole(wup), whole(bgk)],
        out_specs=[row(nq), row(nq), row(nv), row(nv), row(nq), row(nd), row(nd), row(ndv)],
        out_shape=out_shapes,
        compiler_params=pltpu.CompilerParams(
            dimension_semantics=("parallel",), vmem_limit_bytes=VMEM_LIMIT_BYTES),
        name="inproj",
    )(x2, g, wg, wl, wd, wup, bgk)


def _gla_kernel(q_ref, k_ref, v_ref, la_ref, o_ref, *, seq):
    c = GLA_CHUNK
    row_i = lax.broadcasted_iota(jnp.int32, (c, c), 0)
    col_i = lax.broadcasted_iota(jnp.int32, (c, c), 1)
    tril = row_i >= col_i
    tril_ones = jnp.where(tril, 1.0, 0.0).astype(bf16)
    lane = lax.broadcasted_iota(jnp.int32, (c, LANES), 1)
    head0_lanes = lane < GLA_DK
    srow = lax.broadcasted_iota(jnp.int32, (2 * GLA_DV, LANES), 0)
    slane = lax.broadcasted_iota(jnp.int32, (2 * GLA_DV, LANES), 1)
    state_mask = (srow < GLA_DV) == (slane < GLA_DK)

    nt = (((1,), (1,)), ((), ()))
    tn = (((0,), (0,)), ((), ()))
    nb = GLA_BLOCK_CHUNKS
    n_blocks = seq // (nb * c)

    def stage_a(pair, blk):
        rows = [slice((blk * nb + n) * c, (blk * nb + n + 1) * c) for n in range(nb)]
        kc = slice(pair * LANES, (pair + 1) * LANES)
        cum = []
        for n in range(nb):
            la = la_ref[rows[n], kc]
            la_hi = la.astype(bf16)
            la_lo = (la - la_hi.astype(f32)).astype(bf16)
            cum.append(jnp.dot(tril_ones, la_hi, preferred_element_type=f32)
                       + jnp.dot(tril_ones, la_lo, preferred_element_type=f32))
        return pair, rows, cum

    def stage_b(pair, rows, cum):
        kc = slice(pair * LANES, (pair + 1) * LANES)
        vc = slice(pair * 2 * GLA_DV, (pair + 1) * 2 * GLA_DV)
        q_heads, k_intra, k_state, q_inter, decay = [], [], [], [], []
        for n in range(nb):
            b = cum[n]
            b_last = b[c - 1:c, :]
            b_mid = b[c // 2 - 1:c // 2, :]
            q = q_ref[rows[n], kc].astype(f32)
            k = k_ref[rows[n], kc].astype(f32)
            qi = (q * jnp.exp2(b - b_mid)).astype(bf16)
            zero = jnp.zeros_like(qi)
            q_heads.append(jnp.concatenate([jnp.where(head0_lanes, qi, zero),
                                            jnp.where(head0_lanes, zero, qi)], axis=0))
            k_intra.append((k * jnp.exp2(b_mid - b)).astype(bf16))
            k_state.append((k * jnp.exp2(b_last - b)).astype(bf16))
            q_inter.append((q * jnp.exp2(b)).astype(bf16))
            decay.append(jnp.exp2(b_last))
        scores = [lax.dot_general(q_heads[n], k_intra[n], nt, preferred_element_type=f32)
                  for n in range(nb)]
        u_t = [lax.dot_general(v_ref[rows[n], vc], k_state[n], tn, preferred_element_type=f32)
               for n in range(nb)]
        return scores, u_t, q_inter, decay

    def stage_c(pair, rows, scores, u_t, q_inter, decay, state_t):
        states = []
        for n in range(nb):
            states.append(state_t.astype(bf16))
            state_t = state_t * decay[n] + jnp.where(state_mask, u_t[n], 0.0)
        o_inter = [lax.dot_general(q_inter[n], states[n], nt, preferred_element_type=f32)
                   for n in range(nb)]
        for n in range(nb):
            outs = []
            for hh in range(2):
                cols = slice(hh * GLA_DV, (hh + 1) * GLA_DV)
                vcols = slice((2 * pair + hh) * GLA_DV, (2 * pair + hh + 1) * GLA_DV)
                s_h = jnp.where(tril, scores[n][hh * c:(hh + 1) * c, :], 0.0).astype(bf16)
                o_h = o_inter[n][:, cols] + jnp.dot(s_h, v_ref[rows[n], vcols],
                                                    preferred_element_type=f32)
                outs.append(o_h.astype(bf16))
            o_ref[rows[n], slice(pair * 2 * GLA_DV, (pair + 1) * 2 * GLA_DV)] = (
                jnp.concatenate(outs, axis=1))
        return state_t

    items = [(pair, blk) for pair in range(GLA_HEADS // 2) for blk in range(n_blocks)]
    ahead = stage_a(*items[0])
    for n, (pair, blk) in enumerate(items):
        if blk == 0:
            state_t = jnp.zeros((2 * GLA_DV, LANES), f32)
        cur = ahead
        mid = stage_b(*cur)
        if n + 1 < len(items):
            ahead = stage_a(*items[n + 1])
        state_t = stage_c(cur[0], cur[1], *mid, state_t)


def _gla(gq, gk, gv, loga, batch, seq):
    m = gq.shape[0]
    kern = functools.partial(_gla_kernel, seq=seq)

    def rows_of(n):
        return pl.BlockSpec((seq, n), lambda b: (b, 0))

    return pl.pallas_call(
        kern,
        grid=(batch,),
        in_specs=[rows_of(gq.shape[1]), rows_of(gk.shape[1]), rows_of(gv.shape[1]),
                  rows_of(loga.shape[1])],
        out_specs=rows_of(GLA_HEADS * GLA_DV),
        out_shape=jax.ShapeDtypeStruct((m, GLA_HEADS * GLA_DV), bf16),
        compiler_params=pltpu.CompilerParams(
            dimension_semantics=("parallel",), vmem_limit_bytes=VMEM_LIMIT_BYTES),
        name="gla",
    )(gq, gk, gv, loga)


def _diff_kernel(q_ref, k_ref, v_ref, bias_ref, lq1_ref, lk1_ref, lq2_ref, lk2_ref, g_ref,
                 o_ref, vext_ref, *, lambda_init, seq):
    t = T_ATTN
    nq = seq // t
    heads = ATTN_HEADS_PER_STEP
    for hd in range(heads):
        vext_ref[hd, :, :DIFF_DV] = v_ref[:, hd * DIFF_DV:(hd + 1) * DIFF_DV]
        vext_ref[hd, :, DIFF_DV:] = jnp.ones((seq, DIFF_DV), bf16)
    lane = lax.broadcasted_iota(jnp.int32, (t, LANES), 1)
    map1_lanes = lane < DIFF_DK
    lam = (jnp.exp(jnp.sum(lq1_ref[...] * lk1_ref[...], axis=-1, keepdims=True))
           - jnp.exp(jnp.sum(lq2_ref[...] * lk2_ref[...], axis=-1, keepdims=True))
           + lambda_init)
    g = g_ref[...]
    nt = (((1,), (1,)), ((), ()))

    def key_ranges(i):
        pieces = []
        if i >= 2:
            pieces.append((0, (i - 1) * t, None))
        if i >= 1:
            pieces.append(((i - 1) * t, i * t, 1))
        pieces.append((i * t, (i + 1) * t, 0))
        return pieces

    def qk(hd, i):
        cols = slice(hd * LANES, (hd + 1) * LANES)
        q = q_ref[i * t:(i + 1) * t, cols]
        zero = jnp.zeros_like(q)
        qs = jnp.concatenate([jnp.where(map1_lanes, q, zero),
                              jnp.where(map1_lanes, zero, q)], axis=0)
        logits = []
        for lo, hi, bias in key_ranges(i):
            s = lax.dot_general(qs, k_ref[lo:hi, cols], nt, preferred_element_type=f32)
            if bias is not None:
                tile = bias_ref[hd, bias]
                s = s + jnp.concatenate([tile, tile], axis=0)
            logits.append(s)
        return logits

    units = [(hd, i) for hd in range(heads) for i in range(nq)]
    logits = qk(*units[0])
    for n, (hd, i) in enumerate(units):
        nxt = qk(*units[n + 1]) if n + 1 < len(units) else None
        m = functools.reduce(jnp.maximum, [jnp.max(s, axis=-1, keepdims=True) for s in logits])
        pv = None
        for (lo, hi, _), s in zip(key_ranges(i), logits):
            p = jnp.exp2(s - m).astype(bf16)
            part = jnp.dot(p, vext_ref[hd, lo:hi, :], preferred_element_type=f32)
            pv = part if pv is None else pv + part
        o = pv[:, :DIFF_DV] / pv[:, DIFF_DV:]
        o = o[:t, :] - lam * o[t:, :]
        o_ref[i * t:(i + 1) * t, hd * DIFF_DV:(hd + 1) * DIFF_DV] = (
            _rms(o, g) * (1.0 - lambda_init)).astype(bf16)
        logits = nxt


def _diff_attn(dq, dk, dv, bias_tiles, lq1, lk1, lq2, lk2, g_head, batch, seq, lambda_init):
    m = dq.shape[0]
    t = T_ATTN
    heads = ATTN_HEADS_PER_STEP
    kern = functools.partial(_diff_kernel, lambda_init=lambda_init, seq=seq)
    vec = pl.BlockSpec((1, DIFF_DK), lambda b, h: (0, 0))
    head_rows = pl.BlockSpec((seq, heads * LANES), lambda b, h: (b, h))
    return pl.pallas_call(
        kern,
        grid=(batch, DIFF_HEADS // heads),
        in_specs=[
            head_rows, head_rows, head_rows,
            pl.BlockSpec((heads, 2, t, t), lambda b, h: (h, 0, 0, 0)),
            vec, vec, vec, vec,
            pl.BlockSpec((1, DIFF_DV), lambda b, h: (0, 0)),
        ],
        out_specs=head_rows,
        out_shape=jax.ShapeDtypeStruct((m, DIFF_HEADS * DIFF_DV), bf16),
        scratch_shapes=[pltpu.VMEM((heads, seq, 2 * DIFF_DV), bf16)],
        compiler_params=pltpu.CompilerParams(
            dimension_semantics=("parallel", "parallel"),
            vmem_limit_bytes=VMEM_LIMIT_BYTES),
        name="diff_attn",
    )(dq, dk, dv, bias_tiles, lq1, lk1, lq2, lk2, g_head)


def _toeplitz_kernel(w_ref, o_ref):
    t = o_ref.shape[-1]
    for kind in range(o_ref.shape[1]):
        rows = jnp.broadcast_to(w_ref[0, kind], (t, 2 * t))
        o_ref[0, kind] = pltpu.roll(rows, 0, 1, stride=1, stride_axis=0)[:, :t]


def _toeplitz(g, t):
    heads, kinds = g.shape[:2]
    w = jnp.concatenate([g[..., 1:t + 1][..., ::-1], g[..., :1], g[..., t + 1:][..., ::-1]],
                        axis=-1)
    return pl.pallas_call(
        _toeplitz_kernel,
        grid=(heads,),
        in_specs=[pl.BlockSpec((1, kinds, 1, 2 * t), lambda h: (h, 0, 0, 0))],
        out_specs=pl.BlockSpec((1, kinds, t, t), lambda h: (h, 0, 0, 0)),
        out_shape=jax.ShapeDtypeStruct((heads, kinds, t, t), f32),
        name="bias_tiles",
    )(w[:, :, None, :])


def _t5_bias_tiles(rel_bias, t):
    rel = jnp.arange(2 * t)
    max_exact = REL_BUCKETS // 2
    nf = jnp.maximum(rel, 1).astype(f32)
    large = max_exact + (jnp.log(nf / max_exact) / math.log(REL_MAX_DIST / max_exact)
                         * (REL_BUCKETS - max_exact)).astype(jnp.int32)
    large = jnp.minimum(large, REL_BUCKETS - 1)
    bucket = jnp.where(rel < max_exact, rel, large)
    table = rel_bias.astype(f32)
    by_dist = ((table[bucket] - table[REL_BUCKETS - 1][None, :]) * LOG2E).T
    masked = jnp.full((DIFF_HEADS, t), NEG_BIG, f32)
    diag = jnp.concatenate([masked, by_dist[:, :t]], axis=-1)
    return _toeplitz(jnp.stack([diag, by_dist], axis=1), t)


def _outffn_kernel(x_ref, gla_ref, gog_ref, diff_ref, g_gla_ref, wo_ref, post_mix_ref, pre_ffn_ref,
                   wgate_ref, wup_ref, wdown_ref, post_ffn_ref, o_ref):
    bounds = (0,) + FFN_ROW_SPLITS + (x_ref.shape[0],)
    rows = [slice(lo, hi) for lo, hi in zip(bounds[:-1], bounds[1:])]
    pairs = [rows[a:a + 2] for a in range(0, len(rows), 2)]
    n_gla = gla_ref.shape[1]

    def gla_gated(rw):
        heads = []
        for hh in range(GLA_HEADS):
            cols = slice(hh * GLA_DV, (hh + 1) * GLA_DV)
            og = gog_ref[rw, cols].astype(f32)
            silu = og / (1.0 + jnp.exp(-og))
            heads.append((_rms(gla_ref[rw, cols].astype(f32), g_gla_ref[...]) * silu).astype(bf16))
        return jnp.concatenate(heads, axis=1)

    def out_proj(pair):
        return [jnp.dot(diff_ref[rw, :], wo_ref[n_gla:, :], preferred_element_type=f32)
                + jnp.dot(gla_gated(rw), wo_ref[:n_gla, :], preferred_element_type=f32)
                for rw in pair]

    def gate_and_up(ha):
        return (jnp.dot(ha, wgate_ref[...], preferred_element_type=f32),
                jnp.dot(ha, wup_ref[...], preferred_element_type=f32))

    mix = out_proj(pairs[0])
    for n, pair in enumerate(pairs):
        x1 = [x_ref[rw, :] + _rms(mx, post_mix_ref[...]) for rw, mx in zip(pair, mix)]
        h = [_rms(xa, pre_ffn_ref[...]).astype(bf16) for xa in x1]
        gate_up = [gate_and_up(h[0])]
        if n + 1 < len(pairs):
            mix = out_proj(pairs[n + 1])
        gate_up.append(gate_and_up(h[1]))
        f = [(gate / (1.0 + jnp.exp(-gate)) * up).astype(bf16) for gate, up in gate_up]
        y = [jnp.dot(fa, wdown_ref[...], preferred_element_type=f32) for fa in f]
        for rw, xa, ya in zip(pair, x1, y):
            o_ref[rw, :] = xa + _rms(ya, post_ffn_ref[...])


def _outffn(x2, gla_o, gog, diff_o, g_gla, wo, post_mix, pre_ffn, wgate, wup, wdown, post_ffn):
    m, d = x2.shape
    tm = TM_FFN

    def row(n):
        return pl.BlockSpec((tm, n), lambda i: (i, 0))

    def whole(a):
        return pl.BlockSpec(a.shape, lambda i: (0, 0), pipeline_mode=pl.Buffered(1))

    return pl.pallas_call(
        _outffn_kernel,
        grid=(m // tm,),
        in_specs=[row(d), row(gla_o.shape[1]), row(gog.shape[1]), row(diff_o.shape[1]),
                  whole(g_gla), whole(wo), whole(post_mix), whole(pre_ffn),
                  whole(wgate), whole(wup), whole(wdown), whole(post_ffn)],
        out_specs=row(d),
        out_shape=jax.ShapeDtypeStruct((m, d), f32),
        compiler_params=pltpu.CompilerParams(
            dimension_semantics=("parallel",), vmem_limit_bytes=VMEM_LIMIT_BYTES),
        name="outffn",
    )(x2, gla_o, gog, diff_o, g_gla, wo, post_mix, pre_ffn, wgate, wup, wdown, post_ffn)


def kernel(x, w_in, w_gk_up, b_gk, lambda_q1, lambda_k1, lambda_q2, lambda_k2, rel_bias,
           g_gla_head, g_diff_head, w_out, pre_mix_g, post_mix_g, w_gate, w_up, w_down,
           pre_ffn_g, post_ffn_g):
    batch, seq, d = x.shape
    depth = w_in.shape[0]
    assert seq % T_ATTN == 0 and T_ATTN >= REL_MAX_DIST
    assert (batch * seq) % TM_INPROJ == 0 and (batch * seq) % TM_FFN == 0
    assert seq % (GLA_BLOCK_CHUNKS * GLA_CHUNK) == 0
    n_gla = 2 * GLA_HEADS * GLA_DK + 2 * GLA_HEADS * GLA_DV
    bias_tiles = _t5_bias_tiles(rel_bias, T_ATTN)
    x2 = x.reshape(batch * seq, d)
    for l in range(depth):
        lambda_init = 0.8 - 0.6 * math.exp(-0.3 * l)
        wg = w_in[l, :, :n_gla].astype(bf16)
        wl = jnp.pad(w_in[l, :, n_gla:n_gla + GLA_GATE_RANK],
                     ((0, 0), (0, LANES - GLA_GATE_RANK))).astype(bf16)
        wd = w_in[l, :, n_gla + GLA_GATE_RANK:].astype(bf16)
        wup = jnp.pad(w_gk_up[l], ((0, LANES - GLA_GATE_RANK), (0, 0))).astype(bf16)
        gq, gk, gv, gog, loga, dq, dk, dv = _inproj(
            x2, pre_mix_g[l][None, :], wg, wl, wd, wup, b_gk[l][None, :])
        gla_o = _gla(gq, gk, gv, loga, batch, seq)
        diff_o = _diff_attn(dq, dk, dv, bias_tiles,
                            lambda_q1[l][None, :], lambda_k1[l][None, :],
                            lambda_q2[l][None, :], lambda_k2[l][None, :],
                            g_diff_head[l][None, :], batch, seq, lambda_init)
        x2 = _outffn(x2, gla_o, gog, diff_o, g_gla_head[l][None, :], w_out[l].astype(bf16),
                     post_mix_g[l][None, :], pre_ffn_g[l][None, :],
                     w_gate[l].astype(bf16), w_up[l].astype(bf16), w_down[l].astype(bf16),
                     post_ffn_g[l][None, :])
    return x2.reshape(batch, seq, d)
```
